```python
import jax
import jax.numpy as jnp
from jax import lax
import numpy as np

D_MODEL = 4096
BATCH = 4
SEQ = 4096
DEPTH = 2

GRID_W = 64
CTX_LEN = 256
HEAD_DIM = 128
N_Q_HEADS = 16
N_KV_HEADS = 4
Q_PER_KV = N_Q_HEADS // N_KV_HEADS
D_ATTN = N_Q_HEADS * HEAD_DIM
D_KV = N_KV_HEADS * HEAD_DIM
D_SCONV = D_MODEL // 4
D_CCONV = D_MODEL // 4
D_MIX = D_ATTN + D_SCONV + D_CCONV
D_IN = D_ATTN + 2 * D_KV + 3 * D_SCONV + 2 * D_CCONV
SCONV_WIDTH = 3
CCONV_WIDTH = 31
D_FF = 2 * D_MODEL
Q_BLOCK = 128
ROPE_THETA = 10000.0
ROPE_AXIS_DIM = HEAD_DIM // 2
ATTN_SCALE = HEAD_DIM ** -0.5
N_MOD = 9
EPS = 1e-6
ALPHA = (2.0 * DEPTH) ** 0.25
BETA = (8.0 * DEPTH) ** -0.25
ADA_INIT = 0.5

SPLIT_IDX = (
    D_ATTN,
    D_ATTN + D_KV,
    D_ATTN + 2 * D_KV,
    D_ATTN + 2 * D_KV + D_SCONV,
    D_ATTN + 2 * D_KV + 2 * D_SCONV,
    D_ATTN + 2 * D_KV + 3 * D_SCONV,
)

kernel_name = "hybrid_dit_attn_shortconv_conformer"


def layer_norm(x, g=None, b=None):
    xf = x.astype(jnp.float32)
    mu = xf.mean(-1, keepdims=True)
    var = jnp.square(xf - mu).mean(-1, keepdims=True)
    y = (xf - mu) * lax.rsqrt(var + EPS)
    if g is not None:
        y = y * g.astype(jnp.float32) + b.astype(jnp.float32)
    return y.astype(x.dtype)


def rms_norm(x, g):
    xf = x.astype(jnp.float32)
    y = xf * lax.rsqrt(jnp.mean(xf * xf, -1, keepdims=True) + EPS) * g.astype(jnp.float32)
    return y.astype(x.dtype)


def rope_tables(seq_len):
    rows = seq_len // GRID_W
    row = jnp.repeat(jnp.arange(rows, dtype=jnp.float32), GRID_W)
    col = jnp.tile(jnp.arange(GRID_W, dtype=jnp.float32), rows)
    inv_freq = ROPE_THETA ** (-jnp.arange(0, ROPE_AXIS_DIM, 2, dtype=jnp.float32) / ROPE_AXIS_DIM)
    ang = jnp.concatenate([row[:, None] * inv_freq, col[:, None] * inv_freq], -1)
    return jnp.cos(ang), jnp.sin(ang)


def apply_rope(x, cos, sin):
    B, S, H, D = x.shape
    half = ROPE_AXIS_DIM // 2
    xs = x.reshape(B, S, H, 2, 2, half)
    x1, x2 = xs[..., 0, :], xs[..., 1, :]
    c = cos.reshape(S, 1, 2, half).astype(x.dtype)
    s = sin.reshape(S, 1, 2, half).astype(x.dtype)
    out = jnp.stack([x1 * c - x2 * s, x2 * c + x1 * s], axis=-2)
    return out.reshape(B, S, H, D)


def dwconv(x, w):
    pad = w.shape[0] // 2
    return lax.conv_general_dilated(
        x, w[:, None, :], window_strides=(1,), padding=[(pad, pad)],
        dimension_numbers=("NWC", "WIO", "NWC"), feature_group_count=x.shape[-1])


def attend(q, k, v):
    B, Q = q.shape[0], q.shape[1]
    qg = q.reshape(B, Q, N_KV_HEADS, Q_PER_KV, HEAD_DIM)
    s = jnp.einsum("bqkgd,bskd->bkgqs", qg, k, preferred_element_type=jnp.float32) * ATTN_SCALE
    p = jax.nn.softmax(s, axis=-1).astype(v.dtype)
    o = jnp.einsum("bkgqs,bskd->bqkgd", p, v)
    return o.reshape(B, Q, N_Q_HEADS * HEAD_DIM)


def blocked_attention(q, k, v):
    B, S = q.shape[0], q.shape[1]
    nb = S // Q_BLOCK
    qb = q.reshape(B, nb, Q_BLOCK, N_Q_HEADS, HEAD_DIM).transpose(1, 0, 2, 3, 4)
    ob = lax.map(lambda qi: attend(qi, k, v), qb)
    return ob.transpose(1, 0, 2, 3).reshape(B, S, D_ATTN)


def short_conv(sb, sc, sx, w):
    return sb * dwconv(sc * sx, w)


def conformer_conv(cg, w, b, g, beta):
    a, gate = jnp.split(cg, 2, axis=-1)
    u = dwconv(a * jax.nn.sigmoid(gate), w) + b
    return jax.nn.silu(layer_norm(u, g, beta))


def swiglu(h, w_gu, w_down):
    g, u = jnp.split(h @ w_gu, 2, axis=-1)
    return (jax.nn.silu(g) * u) @ w_down


def modulate(x, shift, scale):
    return layer_norm(x) * (1.0 + scale) + shift


def post_norm(x, out, g, b):
    return layer_norm(ALPHA * x + out, g, b)


def token_mixer(h, hc, w_in, w_out, q_g, k_g, sconv_w, cconv_w, cconv_b, cln_g, cln_b,
                cos, sin, with_ctx_out):
    B, S, _ = h.shape
    L = hc.shape[1]
    q, k, v, sb, sc, sx, cg = jnp.split(h @ w_in, SPLIT_IDX, axis=-1)
    q = apply_rope(rms_norm(q.reshape(B, S, N_Q_HEADS, HEAD_DIM), q_g), cos, sin)
    k = apply_rope(rms_norm(k.reshape(B, S, N_KV_HEADS, HEAD_DIM), k_g), cos, sin)
    v = v.reshape(B, S, N_KV_HEADS, HEAD_DIM)

    if with_ctx_out:
        qc, kc, vc, sbc, scc, sxc, cgc = jnp.split(hc @ w_in, SPLIT_IDX, axis=-1)
    else:
        kc, vc = jnp.split(hc @ w_in[:, SPLIT_IDX[0]:SPLIT_IDX[2]], 2, axis=-1)
    kc = rms_norm(kc.reshape(B, L, N_KV_HEADS, HEAD_DIM), k_g)
    vc = vc.reshape(B, L, N_KV_HEADS, HEAD_DIM)

    k_all = jnp.concatenate([kc, k], axis=1)
    v_all = jnp.concatenate([vc, v], axis=1)
    y_attn = blocked_attention(q, k_all, v_all)
    y_sconv = short_conv(sb, sc, sx, sconv_w)
    y_cconv = conformer_conv(cg, cconv_w, cconv_b, cln_g, cln_b)
    y = jnp.concatenate([y_attn, y_sconv, y_cconv], axis=-1) @ w_out

    if not with_ctx_out:
        return y, None
    qc = rms_norm(qc.reshape(B, L, N_Q_HEADS, HEAD_DIM), q_g)
    yc_attn = attend(qc, kc, vc)
    yc_sconv = short_conv(sbc, scc, sxc, sconv_w)
    yc_cconv = conformer_conv(cgc, cconv_w, cconv_b, cln_g, cln_b)
    yc = jnp.concatenate([yc_attn, yc_sconv, yc_cconv], axis=-1) @ w_out
    return y, yc


def setup_inputs(seed: int = 0) -> dict:
    key = jax.random.key(seed)
    ks = jax.random.split(key, 20)

    def nrm(k, shape, scale):
        return jax.random.normal(k, shape, jnp.float32) * scale

    d = D_MODEL
    return {
        "x": nrm(ks[0], (BATCH, SEQ, d), 1.0),
        "c": nrm(ks[1], (BATCH, d), 1.0),
        "ctx": nrm(ks[2], (BATCH, CTX_LEN, d), 1.0),
        "c_ctx": nrm(ks[3], (d,), 1.0),
        "w_ada": nrm(ks[4], (DEPTH, d, N_MOD * d), ADA_INIT * d ** -0.5),
        "b_ada": nrm(ks[5], (DEPTH, N_MOD * d), 0.02),
        "post_ln_g": 1.0 + nrm(ks[6], (DEPTH, 3, d), 0.02),
        "post_ln_b": nrm(ks[7], (DEPTH, 3, d), 0.02),
        "ffn_w_gu": nrm(ks[8], (DEPTH, 2, d, 2 * D_FF), d ** -0.5),
        "ffn_w_down": nrm(ks[9], (DEPTH, 2, D_FF, d), BETA * D_FF ** -0.5),
        "w_in": nrm(ks[10], (DEPTH, d, D_IN), d ** -0.5),
        "w_out": nrm(ks[11], (DEPTH, D_MIX, d), BETA * D_MIX ** -0.5),
        "q_norm_g": 1.0 + nrm(ks[12], (DEPTH, HEAD_DIM), 0.02),
        "k_norm_g": 1.0 + nrm(ks[13], (DEPTH, HEAD_DIM), 0.02),
        "sconv_w": nrm(ks[14], (DEPTH, SCONV_WIDTH, D_SCONV), SCONV_WIDTH ** -0.5),
        "cconv_w": nrm(ks[15], (DEPTH, CCONV_WIDTH, D_CCONV), CCONV_WIDTH ** -0.5),
        "cconv_b": nrm(ks[16], (DEPTH, D_CCONV), 0.02),
        "cconv_ln_g": 1.0 + nrm(ks[17], (DEPTH, D_CCONV), 0.02),
        "cconv_ln_b": nrm(ks[18], (DEPTH, D_CCONV), 0.02),
    }


def reference(x, c, ctx, c_ctx, w_ada, b_ada, post_ln_g, post_ln_b, ffn_w_gu, ffn_w_down,
              w_in, w_out, q_norm_g, k_norm_g, sconv_w, cconv_w, cconv_b, cconv_ln_g,
              cconv_ln_b):
    S = x.shape[1]
    cos, sin = rope_tables(S)
    silu_c = jax.nn.silu(c)
    silu_cc = jax.nn.silu(c_ctx)
    for l in range(DEPTH):
        last = l == DEPTH - 1
        m = (silu_c @ w_ada[l] + b_ada[l]).reshape(-1, N_MOD, D_MODEL)[:, :, None, :]
        mc = (silu_cc @ w_ada[l] + b_ada[l]).reshape(N_MOD, D_MODEL)[:, None, None, :]

        x = post_norm(x, 0.5 * m[:, 2] * swiglu(modulate(x, m[:, 0], m[:, 1]),
                                                ffn_w_gu[l, 0], ffn_w_down[l, 0]),
                      post_ln_g[l, 0], post_ln_b[l, 0])
        ctx = post_norm(ctx, 0.5 * mc[2] * swiglu(modulate(ctx, mc[0], mc[1]),
                                                  ffn_w_gu[l, 0], ffn_w_down[l, 0]),
                        post_ln_g[l, 0], post_ln_b[l, 0])

        y, yc = token_mixer(modulate(x, m[:, 3], m[:, 4]), modulate(ctx, mc[3], mc[4]),
                            w_in[l], w_out[l], q_norm_g[l], k_norm_g[l], sconv_w[l],
                            cconv_w[l], cconv_b[l], cconv_ln_g[l], cconv_ln_b[l],
                            cos, sin, not last)
        x = post_norm(x, m[:, 5] * y, post_ln_g[l, 1], post_ln_b[l, 1])

        x = post_norm(x, 0.5 * m[:, 8] * swiglu(modulate(x, m[:, 6], m[:, 7]),
                                                ffn_w_gu[l, 1], ffn_w_down[l, 1]),
                      post_ln_g[l, 2], post_ln_b[l, 2])
        if not last:
            ctx = post_norm(ctx, mc[5] * yc, post_ln_g[l, 1], post_ln_b[l, 1])
            ctx = post_norm(ctx, 0.5 * mc[8] * swiglu(modulate(ctx, mc[6], mc[7]),
                                                      ffn_w_gu[l, 1], ffn_w_down[l, 1]),
                            post_ln_g[l, 2], post_ln_b[l, 2])
    return x
```

```python
import functools

import jax
import jax.numpy as jnp
from jax import lax
from jax.experimental import pallas as pl
from jax.experimental.pallas import tpu as pltpu

F32 = jnp.float32
BF16 = jnp.bfloat16

HEAD_DIM = 128
Q_PER_KV = 4
GRID_W = 64
ROPE_THETA = 10000.0
SCONV_WIDTH = 3
CCONV_WIDTH = 31
N_MOD = 9
EPS = 1e-6
HALO = 16
V7X_VMEM_LIMIT = 60000 * 1024


def _cparams(sem):
    return pltpu.CompilerParams(dimension_semantics=sem, vmem_limit_bytes=V7X_VMEM_LIMIT)


def _silu(v):
    return v * jax.nn.sigmoid(v)


def _ln_rows(z):
    mu = jnp.mean(z, axis=-1, keepdims=True)
    zc = z - mu
    var = jnp.mean(zc * zc, axis=-1, keepdims=True)
    return zc * lax.rsqrt(var + EPS)


def _adaln_kernel(c_ref, w_ref, b_ref, o_ref):
    sc = _silu(c_ref[...]).astype(BF16)
    o_ref[...] = jnp.dot(sc, w_ref[...].astype(BF16), preferred_element_type=F32) + b_ref[...]


def _adaln(cvec, w_ada, b_ada, tn=512):
    depth, d, n = w_ada.shape
    return pl.pallas_call(
        _adaln_kernel,
        grid=(depth, n // tn),
        in_specs=[pl.BlockSpec((8, d), lambda l, j: (0, 0)),
                  pl.BlockSpec((None, d, tn), lambda l, j: (l, 0, j)),
                  pl.BlockSpec((None, 1, tn), lambda l, j: (l, 0, j))],
        out_specs=pl.BlockSpec((None, 8, tn), lambda l, j: (l, 0, j)),
        out_shape=jax.ShapeDtypeStruct((depth, 8, n), F32),
        compiler_params=_cparams(("arbitrary", "arbitrary")),
        name="adaln",
    )(cvec, w_ada, b_ada.reshape(depth, 1, n))


def _lnmod_kernel(x_ref, mod_ref, h_ref, *, shift_row, rows):
    shift = mod_ref[shift_row:shift_row + 1, :]
    scale1 = 1.0 + mod_ref[shift_row + 1:shift_row + 2, :]

    def body(r, carry):
        sl = pl.ds(pl.multiple_of(r * rows, rows), rows)
        h_ref[sl, :] = (_ln_rows(x_ref[sl, :]) * scale1 + shift).astype(BF16)
        return carry

    lax.fori_loop(0, x_ref.shape[0] // rows, body, 0)


def _lnmod(x, mod, group_of, shift_row, tm):
    t, d = x.shape
    return pl.pallas_call(
        functools.partial(_lnmod_kernel, shift_row=shift_row, rows=64),
        grid=(t // tm,),
        in_specs=[pl.BlockSpec((tm, d), lambda i: (i, 0)),
                  pl.BlockSpec((None, N_MOD, d), lambda i: (group_of(i, tm), 0, 0))],
        out_specs=pl.BlockSpec((tm, d), lambda i: (i, 0)),
        out_shape=jax.ShapeDtypeStruct((t, d), BF16),
        compiler_params=_cparams(("arbitrary",)),
        name="lnmod",
    )(x, mod)


def _glu_kernel(h_ref, wg_ref, wu_ref, o_ref):
    h = h_ref[...]
    g = jnp.dot(h, wg_ref[...], preferred_element_type=F32)
    u = jnp.dot(h, wu_ref[...], preferred_element_type=F32)
    o_ref[...] = (_silu(g) * u).astype(BF16)


def _mm_glu(h, w_gu, tm, tn):
    t, d = h.shape
    f = w_gu.shape[1] // 2
    nj = f // tn
    return pl.pallas_call(
        _glu_kernel,
        grid=(t // tm, nj),
        in_specs=[pl.BlockSpec((tm, d), lambda i, j: (i, 0)),
                  pl.BlockSpec((d, tn), lambda i, j: (0, j)),
                  pl.BlockSpec((d, tn), lambda i, j: (0, j + nj))],
        out_specs=pl.BlockSpec((tm, tn), lambda i, j: (i, j)),
        out_shape=jax.ShapeDtypeStruct((t, f), BF16),
        compiler_params=_cparams(("arbitrary", "arbitrary")),
        name="mm_glu",
    )(h, w_gu, w_gu)


def _plain_kernel(h_ref, w_ref, o_ref):
    o_ref[...] = jnp.dot(h_ref[...], w_ref[...], preferred_element_type=F32).astype(BF16)


def _mm_plain(h, w, tm, tn):
    t, d = h.shape
    n = w.shape[1]
    return pl.pallas_call(
        _plain_kernel,
        grid=(t // tm, n // tn),
        in_specs=[pl.BlockSpec((tm, d), lambda i, j: (i, 0)),
                  pl.BlockSpec((d, tn), lambda i, j: (0, j))],
        out_specs=pl.BlockSpec((tm, tn), lambda i, j: (i, j)),
        out_shape=jax.ShapeDtypeStruct((t, n), BF16),
        compiler_params=_cparams(("arbitrary", "arbitrary")),
        name="mm_in",
    )(h, w)


def _mmres_kernel(*refs, nk, gate_row, gate_mul, alpha, ln_row, next_row, cn, rows, two_a):
    if two_a:
        a_ref, a2_ref = refs[0], refs[1]
        refs = refs[2:]
    else:
        a_ref, a2_ref = refs[0], None
        refs = refs[1:]
    if next_row is None:
        w_ref, x_ref, mod_ref, lng_ref, lnb_ref, xo_ref = refs
        modn_ref = ho_ref = None
    else:
        w_ref, x_ref, mod_ref, modn_ref, lng_ref, lnb_ref, xo_ref, ho_ref = refs
    k = pl.program_id(1)
    n = xo_ref.shape[1]

    @pl.when(k == 0)
    def _():
        xo_ref[...] = jnp.zeros_like(xo_ref)

    def accumulate(a):
        for c in range(n // cn):
            sl = slice(c * cn, (c + 1) * cn)
            xo_ref[:, sl] += jnp.dot(a, w_ref[:, sl], preferred_element_type=F32)

    if two_a:
        @pl.when(k < nk // 2)
        def _():
            accumulate(a_ref[...])

        @pl.when(k >= nk // 2)
        def _():
            accumulate(a2_ref[...])
    else:
        accumulate(a_ref[...])

    @pl.when(k == nk - 1)
    def _():
        gate = mod_ref[gate_row:gate_row + 1, :] * gate_mul
        g = lng_ref[ln_row:ln_row + 1, :]
        b = lnb_ref[ln_row:ln_row + 1, :]
        if next_row is not None:
            shift = modn_ref[next_row:next_row + 1, :]
            scale1 = 1.0 + modn_ref[next_row + 1:next_row + 2, :]

        def body(r, carry):
            sl = pl.ds(pl.multiple_of(r * rows, rows), rows)
            y = _ln_rows(alpha * x_ref[sl, :] + gate * xo_ref[sl, :]) * g + b
            xo_ref[sl, :] = y
            if next_row is not None:
                ho_ref[sl, :] = (_ln_rows(y) * scale1 + shift).astype(BF16)
            return carry

        lax.fori_loop(0, xo_ref.shape[0] // rows, body, 0)


def _mm_res(a, w, x, mod, modn, lng, lnb, group_of, *, gate_row, gate_mul, alpha, ln_row, next_row, tm, tk, a2=None):
    t, d = x.shape
    kdim = w.shape[0]
    nk = kdim // tk
    two_a = a2 is not None
    if two_a:
        half = nk // 2
        a_specs = [pl.BlockSpec((tm, tk), lambda i, k: (i, jnp.minimum(k, half - 1))),
                   pl.BlockSpec((tm, tk), lambda i, k: (i, jnp.maximum(k - half, 0)))]
        a_args = [a, a2]
    else:
        a_specs = [pl.BlockSpec((tm, tk), lambda i, k: (i, k))]
        a_args = [a]
    mod_spec = pl.BlockSpec((None, N_MOD, d), lambda i, k: (group_of(i, tm), 0, 0))
    ln_spec = pl.BlockSpec((3, d), lambda i, k: (0, 0))
    row_spec = pl.BlockSpec((tm, d), lambda i, k: (i, 0))
    in_specs = a_specs + [pl.BlockSpec((tk, d), lambda i, k: (k, 0)), row_spec, mod_spec]
    args = a_args + [w, x, mod]
    if next_row is not None:
        in_specs.append(mod_spec)
        args.append(modn)
        out_specs = [row_spec, row_spec]
        out_shape = [jax.ShapeDtypeStruct((t, d), F32), jax.ShapeDtypeStruct((t, d), BF16)]
    else:
        out_specs = [row_spec]
        out_shape = [jax.ShapeDtypeStruct((t, d), F32)]
    in_specs += [ln_spec, ln_spec]
    args += [lng, lnb]
    kern = functools.partial(_mmres_kernel, nk=nk, gate_row=gate_row, gate_mul=gate_mul, alpha=alpha, ln_row=ln_row,
                             next_row=next_row, cn=min(1024, d), rows=32, two_a=two_a)
    outs = pl.pallas_call(
        kern,
        grid=(t // tm, nk),
        in_specs=in_specs,
        out_specs=out_specs,
        out_shape=out_shape,
        compiler_params=_cparams(("arbitrary", "arbitrary")),
        name="mm_res",
    )(*args)
    return (outs[0], outs[1]) if next_row is not None else (outs[0], None)


def _qkprep_kernel(p_ref, g_ref, cos_ref, sin_ref, o_ref):
    cos = cos_ref[...]
    sin = sin_ref[...]
    lane = lax.broadcasted_iota(jnp.int32, cos.shape, 1)
    lower = (lane & (HEAD_DIM // 4)) == 0
    for hd in range(p_ref.shape[1] // HEAD_DIM):
        sl = slice(hd * HEAD_DIM, (hd + 1) * HEAD_DIM)
        v = p_ref[:, sl].astype(F32)
        y = v * lax.rsqrt(jnp.mean(v * v, axis=-1, keepdims=True) + EPS) * g_ref[:, sl]
        partner = jnp.where(lower, pltpu.roll(y, HEAD_DIM - HEAD_DIM // 4, 1), pltpu.roll(y, HEAD_DIM // 4, 1))
        o_ref[:, sl] = (y * cos + partner * sin).astype(BF16)


def _qk_prep(p, gains, cos, sin, tm):
    t = p.shape[0]
    nqk = gains.shape[1]
    heads = nqk // HEAD_DIM
    tn = HEAD_DIM * max(n for n in (1, 2, 4) if heads % n == 0)
    return pl.pallas_call(
        _qkprep_kernel,
        grid=(t // tm, nqk // tn),
        in_specs=[pl.BlockSpec((tm, tn), lambda i, j: (i, j)),
                  pl.BlockSpec((1, tn), lambda i, j: (0, j)),
                  pl.BlockSpec((tm, HEAD_DIM), lambda i, j: (i, 0)),
                  pl.BlockSpec((tm, HEAD_DIM), lambda i, j: (i, 0))],
        out_specs=pl.BlockSpec((tm, tn), lambda i, j: (i, j)),
        out_shape=jax.ShapeDtypeStruct((t, nqk), BF16),
        compiler_params=_cparams(("arbitrary", "arbitrary")),
        name="qk_prep",
    )(p, gains, cos, sin)


def _attn_kernel(*refs, with_latent):
    if with_latent:
        q_ref, kc_ref, vc_ref, kl_ref, vl_ref, o_ref = refs
    else:
        q_ref, kc_ref, vc_ref, o_ref = refs
    dn = (((1,), (1,)), ((), ()))
    for g in range(Q_PER_KV):
        sl = slice(g * HEAD_DIM, (g + 1) * HEAD_DIM)
        q = q_ref[:, sl]
        s1 = lax.dot_general(q, kc_ref[...], dn, preferred_element_type=F32)
        m = jnp.max(s1, axis=-1, keepdims=True)
        if with_latent:
            s2 = lax.dot_general(q, kl_ref[...], dn, preferred_element_type=F32)
            m = jnp.maximum(m, jnp.max(s2, axis=-1, keepdims=True))
        p1 = jnp.exp(s1 - m)
        l = jnp.sum(p1, axis=-1, keepdims=True)
        o = jnp.dot(p1.astype(BF16), vc_ref[...], preferred_element_type=F32)
        if with_latent:
            p2 = jnp.exp(s2 - m)
            l = l + jnp.sum(p2, axis=-1, keepdims=True)
            o = o + jnp.dot(p2.astype(BF16), vl_ref[...], preferred_element_type=F32)
        o_ref[:, sl] = (o / l).astype(BF16)


def _attention(qk, p, *, b, s, l, n_kv, tq, latent):
    kcol = n_kv * Q_PER_KV
    vcol = kcol + n_kv
    ctx0 = b * s // l
    qw = Q_PER_KV * HEAD_DIM
    if latent:
        nq = s // tq
        grid = (b, n_kv, nq)
        in_specs = [pl.BlockSpec((tq, qw), lambda bi, h, qi: (bi * nq + qi, h)),
                    pl.BlockSpec((l, HEAD_DIM), lambda bi, h, qi: (ctx0 + bi, kcol + h)),
                    pl.BlockSpec((l, HEAD_DIM), lambda bi, h, qi: (ctx0 + bi, vcol + h)),
                    pl.BlockSpec((s, HEAD_DIM), lambda bi, h, qi: (bi, kcol + h)),
                    pl.BlockSpec((s, HEAD_DIM), lambda bi, h, qi: (bi, vcol + h))]
        args = (qk, qk, p, qk, p)
        out_specs = pl.BlockSpec((tq, qw), lambda bi, h, qi: (bi * nq + qi, h))
        rows = b * s
        sem = ("arbitrary", "arbitrary", "arbitrary")
    else:
        grid = (b, n_kv)
        in_specs = [pl.BlockSpec((l, qw), lambda bi, h: (ctx0 + bi, h)),
                    pl.BlockSpec((l, HEAD_DIM), lambda bi, h: (ctx0 + bi, kcol + h)),
                    pl.BlockSpec((l, HEAD_DIM), lambda bi, h: (ctx0 + bi, vcol + h))]
        args = (qk, qk, p)
        out_specs = pl.BlockSpec((l, qw), lambda bi, h: (bi, h))
        rows = b * l
        sem = ("arbitrary", "arbitrary")
    return pl.pallas_call(
        functools.partial(_attn_kernel, with_latent=latent),
        grid=grid,
        in_specs=in_specs,
        out_specs=out_specs,
        out_shape=jax.ShapeDtypeStruct((rows, n_kv * qw), BF16),
        compiler_params=_cparams(sem),
        name="attn_lat" if latent else "attn_ctx",
    )(*args)


def _conv_kernel(sb_ref, sc_ref, sx_ref, a_ref, gt_ref,
                 scp_ref, sxp_ref, ap_ref, gtp_ref, scn_ref, sxn_ref, an_ref, gtn_ref,
                 sw_ref, cw_ref, cb_ref, cg_ref, cbeta_ref, os_ref, oc_ref, z_scr, u_scr,
                 *, ts, lat_rows, s, l, rc):
    i = pl.program_id(0)
    row0 = i * ts
    is_ctx = row0 >= lat_rows
    pos = jnp.where(is_ctx, lax.rem(jnp.maximum(row0 - lat_rows, 0), l), lax.rem(row0, s))
    seqlen = jnp.where(is_ctx, l, s)
    keep_prev = (pos > 0).astype(F32)
    keep_next = (pos + ts < seqlen).astype(F32)

    def glu(a, g):
        return a.astype(F32) * jax.nn.sigmoid(g.astype(F32))

    z_scr[0:HALO, :] = scp_ref[...].astype(F32) * sxp_ref[...].astype(F32) * keep_prev
    z_scr[HALO:HALO + ts, :] = sc_ref[...].astype(F32) * sx_ref[...].astype(F32)
    z_scr[HALO + ts:2 * HALO + ts, :] = scn_ref[...].astype(F32) * sxn_ref[...].astype(F32) * keep_next
    u_scr[0:HALO, :] = glu(ap_ref[...], gtp_ref[...]) * keep_prev
    u_scr[HALO:HALO + ts, :] = glu(a_ref[...], gt_ref[...])
    u_scr[HALO + ts:2 * HALO + ts, :] = glu(an_ref[...], gtn_ref[...]) * keep_next

    spad = SCONV_WIDTH // 2
    cpad = CCONV_WIDTH // 2
    for r in range(ts // rc):
        base = HALO + r * rc
        acc = sw_ref[0:1, :] * z_scr[base - spad:base - spad + rc, :]
        for k in range(1, SCONV_WIDTH):
            acc = acc + sw_ref[k:k + 1, :] * z_scr[base - spad + k:base - spad + k + rc, :]
        os_ref[r * rc:(r + 1) * rc, :] = (sb_ref[r * rc:(r + 1) * rc, :].astype(F32) * acc).astype(BF16)

        acc = cw_ref[0:1, :] * u_scr[base - cpad:base - cpad + rc, :]
        for k in range(1, CCONV_WIDTH):
            acc = acc + cw_ref[k:k + 1, :] * u_scr[base - cpad + k:base - cpad + k + rc, :]
        y = _ln_rows(acc + cb_ref[...]) * cg_ref[...] + cbeta_ref[...]
        oc_ref[r * rc:(r + 1) * rc, :] = _silu(y).astype(BF16)


def _convs(p, sconv_w, cconv_w, cconv_b, cln_g, cln_b, *, b, s, l, ts):
    t = p.shape[0]
    c = sconv_w.shape[1]
    hb = ts // HALO
    nhb = t // HALO
    main = lambda col: pl.BlockSpec((ts, c), lambda i: (i, col))
    prev = lambda col: pl.BlockSpec((HALO, c), lambda i: (jnp.maximum(i * hb - 1, 0), col))
    nxt = lambda col: pl.BlockSpec((HALO, c), lambda i: (jnp.minimum((i + 1) * hb, nhb - 1), col))
    vec = lambda r: pl.BlockSpec((r, c), lambda i: (0, 0))
    in_specs = ([main(3), main(4), main(5), main(6), main(7)]
                + [prev(4), prev(5), prev(6), prev(7)] + [nxt(4), nxt(5), nxt(6), nxt(7)]
                + [vec(SCONV_WIDTH), vec(CCONV_WIDTH), vec(1), vec(1), vec(1)])
    out_spec = pl.BlockSpec((ts, c), lambda i: (i, 0))
    kern = functools.partial(_conv_kernel, ts=ts, lat_rows=b * s, s=s, l=l, rc=32)
    return pl.pallas_call(
        kern,
        grid=(t // ts,),
        in_specs=in_specs,
        out_specs=[out_spec, out_spec],
        out_shape=[jax.ShapeDtypeStruct((t, c), BF16), jax.ShapeDtypeStruct((t, c), BF16)],
        scratch_shapes=[pltpu.VMEM((ts + 2 * HALO, c), F32), pltpu.VMEM((ts + 2 * HALO, c), F32)],
        compiler_params=_cparams(("arbitrary",)),
        name="convs",
    )(*([p] * 13), sconv_w, cconv_w, cconv_b.reshape(1, c), cln_g.reshape(1, c), cln_b.reshape(1, c))


def _rope_tables(b, s, l):
    rows = s // GRID_W
    row = jnp.repeat(jnp.arange(rows, dtype=F32), GRID_W)
    col = jnp.tile(jnp.arange(GRID_W, dtype=F32), rows)
    axis_dim = HEAD_DIM // 2
    inv_freq = ROPE_THETA ** (-jnp.arange(0, axis_dim, 2, dtype=F32) / axis_dim)
    ar, ac = row[:, None] * inv_freq, col[:, None] * inv_freq
    cos = jnp.concatenate([jnp.cos(ar), jnp.cos(ar), jnp.cos(ac), jnp.cos(ac)], -1)
    sin = jnp.concatenate([-jnp.sin(ar), jnp.sin(ar), -jnp.sin(ac), jnp.sin(ac)], -1)
    cos = jnp.concatenate([jnp.tile(cos, (b, 1)), jnp.ones((b * l, HEAD_DIM), F32)], 0)
    sin = jnp.concatenate([jnp.tile(sin, (b, 1)), jnp.zeros((b * l, HEAD_DIM), F32)], 0)
    return cos, sin


def kernel(x, c, ctx, c_ctx, w_ada, b_ada, post_ln_g, post_ln_b, ffn_w_gu, ffn_w_down, w_in, w_out, q_norm_g,
           k_norm_g, sconv_w, cconv_w, cconv_b, cconv_ln_g, cconv_ln_b):
    b, s, d = x.shape
    l = ctx.shape[1]
    depth = w_ada.shape[0]
    n_q = d // (2 * HEAD_DIM)
    n_kv = n_q // Q_PER_KV
    alpha = (2.0 * depth) ** 0.25
    attn_scale = HEAD_DIM ** -0.5
    t = b * s + b * l
    tm = min(1024, s, b * l)
    tmr = min(512, tm)
    ts = min(256, l)
    assert s % tm == 0 and (b * l) % tm == 0 and s % ts == 0 and l % ts == 0 and s % l == 0
    assert ts % HALO == 0 and HALO > CCONV_WIDTH // 2

    def group_of(i, tile):
        return jnp.minimum(i * tile // s, b)

    xs = jnp.concatenate([x.reshape(b * s, d), ctx.reshape(b * l, d)], 0)
    cvec = jnp.zeros((8, d), F32).at[:b].set(c).at[b].set(c_ctx)
    mods = _adaln(cvec, w_ada, b_ada).reshape(depth, 8, N_MOD, d)
    cos, sin = _rope_tables(b, s, l)
    w_gu16, w_dn16 = ffn_w_gu.astype(BF16), ffn_w_down.astype(BF16)
    w_in16, w_out16 = w_in.astype(BF16), w_out.astype(BF16)

    h = _lnmod(xs, mods[0], group_of, 0, tmr)
    for li in range(depth):
        mod = mods[li]
        modn = mods[li + 1] if li + 1 < depth else None
        lng, lnb = post_ln_g[li], post_ln_b[li]
        a = _mm_glu(h, w_gu16[li, 0], tm, 512)
        xs, h = _mm_res(a, w_dn16[li, 0], xs, mod, mod, lng, lnb, group_of, gate_row=2, gate_mul=0.5, alpha=alpha,
                        ln_row=0, next_row=3, tm=tmr, tk=512)

        p = _mm_plain(h, w_in16[li], tm, 512)
        gains = jnp.concatenate([jnp.tile(q_norm_g[li] * attn_scale, n_q), jnp.tile(k_norm_g[li], n_kv)])[None, :]
        qk = _qk_prep(p, gains, cos, sin, tm)
        o_lat = _attention(qk, p, b=b, s=s, l=l, n_kv=n_kv, tq=min(256, s), latent=True)
        o_ctx = _attention(qk, p, b=b, s=s, l=l, n_kv=n_kv, tq=l, latent=False)
        y_s, y_c = _convs(p, sconv_w[li], cconv_w[li], cconv_b[li], cconv_ln_g[li], cconv_ln_b[li], b=b, s=s, l=l, ts=ts)
        y_attn = jnp.concatenate([o_lat, o_ctx], 0)
        y_conv = jnp.concatenate([y_s, y_c], 1)
        xs, h = _mm_res(y_attn, w_out16[li], xs, mod, mod, lng, lnb, group_of, gate_row=5, gate_mul=1.0, alpha=alpha,
                        ln_row=1, next_row=6, tm=tmr, tk=512, a2=y_conv)

        a = _mm_glu(h, w_gu16[li, 1], tm, 512)
        xs, h = _mm_res(a, w_dn16[li, 1], xs, mod, modn, lng, lnb, group_of, gate_row=8, gate_mul=0.5, alpha=alpha,
                        ln_row=2, next_row=0 if modn is not None else None, tm=tmr, tk=512)
    return xs[:b * s].reshape(b, s, d)
```

```python
import functools
import math

import jax
import jax.numpy as jnp
from jax import lax
from jax.experimental import pallas as pl
from jax.experimental.pallas import tpu as pltpu

F32 = jnp.float32
BF16 = jnp.bfloat16

HEAD_DIM = 128
Q_PER_KV = 4
GRID_W = 64
ROPE_THETA = 10000.0
SCONV_WIDTH = 3
CCONV_WIDTH = 31
N_MOD = 9
EPS = 1e-6
SUBLANES = 8
HALO = 16
V7X_VMEM_LIMIT = 60000 * 1024


def _cparams(sem):
    return pltpu.CompilerParams(dimension_semantics=sem, vmem_limit_bytes=V7X_VMEM_LIMIT)


def _silu(v):
    return v * jax.nn.sigmoid(v)


def _ln_rows(z):
    mu = jnp.mean(z, axis=-1, keepdims=True)
    zc = z - mu
    var = jnp.mean(zc * zc, axis=-1, keepdims=True)
    return zc * lax.rsqrt(var + EPS)


def _stacked(block, lead, index):
    return pl.BlockSpec((None,) * len(lead) + block, lambda *g: lead + index(*g))


def _adaln_kernel(c_ref, w_ref, b_ref, o_ref):
    sc = _silu(c_ref[...]).astype(BF16)
    o_ref[...] = jnp.dot(sc, w_ref[...].astype(BF16), preferred_element_type=F32) + b_ref[...]


def _adaln(cvec, w_ada, b_ada, tn=512):
    depth, d, n = w_ada.shape
    return pl.pallas_call(
        _adaln_kernel,
        grid=(depth, n // tn),
        in_specs=[pl.BlockSpec((8, d), lambda l, j: (0, 0)),
                  pl.BlockSpec((None, d, tn), lambda l, j: (l, 0, j)),
                  pl.BlockSpec((None, 1, tn), lambda l, j: (l, 0, j))],
        out_specs=pl.BlockSpec((None, 8, tn), lambda l, j: (l, 0, j)),
        out_shape=jax.ShapeDtypeStruct((depth, 8, n), F32),
        compiler_params=_cparams(("arbitrary", "arbitrary")),
        name="adaln",
    )(cvec, w_ada, b_ada.reshape(depth, 1, n))


def _lnmod_kernel(x_ref, mod_ref, h_ref, *, shift_row, rows):
    shift = mod_ref[shift_row:shift_row + 1, :]
    scale1 = 1.0 + mod_ref[shift_row + 1:shift_row + 2, :]

    def body(r, carry):
        sl = pl.ds(pl.multiple_of(r * rows, rows), rows)
        h_ref[sl, :] = (_ln_rows(x_ref[sl, :]) * scale1 + shift).astype(BF16)
        return carry

    lax.fori_loop(0, x_ref.shape[0] // rows, body, 0)


def _lnmod(x, mods, li, group_of, shift_row, tm):
    t, d = x.shape
    return pl.pallas_call(
        functools.partial(_lnmod_kernel, shift_row=shift_row, rows=64),
        grid=(t // tm,),
        in_specs=[pl.BlockSpec((tm, d), lambda i: (i, 0)),
                  pl.BlockSpec((None, None, N_MOD, d), lambda i: (li, group_of(i, tm), 0, 0))],
        out_specs=pl.BlockSpec((tm, d), lambda i: (i, 0)),
        out_shape=jax.ShapeDtypeStruct((t, d), BF16),
        compiler_params=_cparams(("arbitrary",)),
        name="lnmod",
    )(x, mods)


def _glu_kernel(h_ref, wg_ref, wu_ref, o_ref):
    h = h_ref[...]
    g = jnp.dot(h, wg_ref[...], preferred_element_type=F32)
    u = jnp.dot(h, wu_ref[...], preferred_element_type=F32)
    o_ref[...] = (_silu(g) * u).astype(BF16)


def _mm_glu(h, w_gu, lead, rows, tm, tn):
    d = h.shape[1]
    f = w_gu.shape[-1] // 2
    nj = f // tn
    return pl.pallas_call(
        _glu_kernel,
        grid=(rows // tm, nj),
        in_specs=[pl.BlockSpec((tm, d), lambda i, j: (i, 0)),
                  _stacked((d, tn), lead, lambda i, j: (0, j)),
                  _stacked((d, tn), lead, lambda i, j: (0, j + nj))],
        out_specs=pl.BlockSpec((tm, tn), lambda i, j: (i, j)),
        out_shape=jax.ShapeDtypeStruct((rows, f), BF16),
        compiler_params=_cparams(("arbitrary", "arbitrary")),
        name="mm_glu",
    )(h, w_gu, w_gu)


def _plain_kernel(h_ref, w_ref, o_ref):
    o_ref[...] = jnp.dot(h_ref[...], w_ref[...], preferred_element_type=F32).astype(BF16)


def _mm_plain(h, w, lead, tm, tn):
    t, d = h.shape
    n = w.shape[-1]
    return pl.pallas_call(
        _plain_kernel,
        grid=(t // tm, n // tn),
        in_specs=[pl.BlockSpec((tm, d), lambda i, j: (i, 0)),
                  _stacked((d, tn), lead, lambda i, j: (0, j))],
        out_specs=pl.BlockSpec((tm, tn), lambda i, j: (i, j)),
        out_shape=jax.ShapeDtypeStruct((t, n), BF16),
        compiler_params=_cparams(("arbitrary", "arbitrary")),
        name="mm_in",
    )(h, w)


def _mmres_kernel(*refs, nk, gate_row, gate_mul, alpha, ln_row, next_row, cn, rows, two_a):
    if two_a:
        a_ref, a2_ref = refs[0], refs[1]
        refs = refs[2:]
    else:
        a_ref, a2_ref = refs[0], None
        refs = refs[1:]
    if next_row is None:
        w_ref, x_ref, mod_ref, lng_ref, lnb_ref, xo_ref = refs
        modn_ref = ho_ref = None
    else:
        w_ref, x_ref, mod_ref, modn_ref, lng_ref, lnb_ref, xo_ref, ho_ref = refs
    k = pl.program_id(1)
    n = xo_ref.shape[1]

    @pl.when(k == 0)
    def _():
        xo_ref[...] = jnp.zeros_like(xo_ref)

    def accumulate(a):
        for c in range(n // cn):
            sl = slice(c * cn, (c + 1) * cn)
            xo_ref[:, sl] += jnp.dot(a, w_ref[:, sl], preferred_element_type=F32)

    if two_a:
        @pl.when(k < nk // 2)
        def _():
            accumulate(a_ref[...])

        @pl.when(k >= nk // 2)
        def _():
            accumulate(a2_ref[...])
    else:
        accumulate(a_ref[...])

    @pl.when(k == nk - 1)
    def _():
        gate = mod_ref[gate_row:gate_row + 1, :] * gate_mul
        g = lng_ref[ln_row:ln_row + 1, :]
        b = lnb_ref[ln_row:ln_row + 1, :]
        if next_row is not None:
            shift = modn_ref[next_row:next_row + 1, :]
            scale1 = 1.0 + modn_ref[next_row + 1:next_row + 2, :]

        def body(r, carry):
            sl = pl.ds(pl.multiple_of(r * rows, rows), rows)
            y = _ln_rows(alpha * x_ref[sl, :] + gate * xo_ref[sl, :]) * g + b
            xo_ref[sl, :] = y
            if next_row is not None:
                ho_ref[sl, :] = (_ln_rows(y) * scale1 + shift).astype(BF16)
            return carry

        lax.fori_loop(0, xo_ref.shape[0] // rows, body, 0, unroll=2)


def _mm_res(a, w, wlead, x, mods, li, lin, post_g, post_b, group_of, *, rows, gate_row, gate_mul, alpha, ln_row,
            next_row, tm, tk, a2=None):
    d = x.shape[1]
    kdim = w.shape[-2]
    nk = kdim // tk
    two_a = a2 is not None
    if two_a:
        half = nk // 2
        a_specs = [pl.BlockSpec((tm, tk), lambda i, k: (i, jnp.minimum(k, half - 1))),
                   pl.BlockSpec((tm, tk), lambda i, k: (i, jnp.maximum(k - half, 0)))]
        a_args = [a, a2]
    else:
        a_specs = [pl.BlockSpec((tm, tk), lambda i, k: (i, k))]
        a_args = [a]
    mod_spec = lambda layer: pl.BlockSpec((None, None, N_MOD, d), lambda i, k: (layer, group_of(i, tm), 0, 0))
    ln_spec = pl.BlockSpec((None, 3, d), lambda i, k: (li, 0, 0))
    row_spec = pl.BlockSpec((tm, d), lambda i, k: (i, 0))
    in_specs = a_specs + [_stacked((tk, d), wlead, lambda i, k: (k, 0)), row_spec, mod_spec(li)]
    args = a_args + [w, x, mods]
    if next_row is not None:
        in_specs.append(mod_spec(lin))
        args.append(mods)
        out_specs = [row_spec, row_spec]
        out_shape = [jax.ShapeDtypeStruct((rows, d), F32), jax.ShapeDtypeStruct((rows, d), BF16)]
    else:
        out_specs = [row_spec]
        out_shape = [jax.ShapeDtypeStruct((rows, d), F32)]
    in_specs += [ln_spec, ln_spec]
    args += [post_g, post_b]
    kern = functools.partial(_mmres_kernel, nk=nk, gate_row=gate_row, gate_mul=gate_mul, alpha=alpha, ln_row=ln_row,
                             next_row=next_row, cn=min(1024, d), rows=32, two_a=two_a)
    outs = pl.pallas_call(
        kern,
        grid=(rows // tm, nk),
        in_specs=in_specs,
        out_specs=out_specs,
        out_shape=out_shape,
        compiler_params=_cparams(("arbitrary", "arbitrary")),
        name="mm_res",
    )(*args)
    return (outs[0], outs[1]) if next_row is not None else (outs[0], None)


def _qkprep_kernel(p_ref, g_ref, cos_ref, sin_ref, o_ref):
    cos = cos_ref[...]
    sin = sin_ref[...]
    lane = lax.broadcasted_iota(jnp.int32, cos.shape, 1)
    lower = (lane & (HEAD_DIM // 4)) == 0
    for hd in range(p_ref.shape[1] // HEAD_DIM):
        sl = slice(hd * HEAD_DIM, (hd + 1) * HEAD_DIM)
        v = p_ref[:, sl].astype(F32)
        y = v * lax.rsqrt(jnp.mean(v * v, axis=-1, keepdims=True) + EPS) * g_ref[:, sl]
        partner = jnp.where(lower, pltpu.roll(y, HEAD_DIM - HEAD_DIM // 4, 1), pltpu.roll(y, HEAD_DIM // 4, 1))
        o_ref[:, sl] = (y * cos + partner * sin).astype(BF16)


def _qk_prep(p, gains, li, cos, sin, tm):
    t = p.shape[0]
    nqk = gains.shape[-1]
    heads = nqk // HEAD_DIM
    tn = HEAD_DIM * max(n for n in (1, 2, 4) if heads % n == 0)
    return pl.pallas_call(
        _qkprep_kernel,
        grid=(t // tm, nqk // tn),
        in_specs=[pl.BlockSpec((tm, tn), lambda i, j: (i, j)),
                  pl.BlockSpec((None, 1, tn), lambda i, j: (li, 0, j)),
                  pl.BlockSpec((tm, HEAD_DIM), lambda i, j: (i, 0)),
                  pl.BlockSpec((tm, HEAD_DIM), lambda i, j: (i, 0))],
        out_specs=pl.BlockSpec((tm, tn), lambda i, j: (i, j)),
        out_shape=jax.ShapeDtypeStruct((t, nqk), BF16),
        compiler_params=_cparams(("arbitrary", "arbitrary")),
        name="qk_prep",
    )(p, gains, cos, sin)


def _attn_kernel(*refs, with_latent):
    if with_latent:
        q_ref, kc_ref, vc_ref, kl_ref, vl_ref, o_ref = refs
    else:
        q_ref, kc_ref, vc_ref, _, o_ref = refs
    dn = (((1,), (1,)), ((), ()))
    for g in range(Q_PER_KV):
        sl = slice(g * HEAD_DIM, (g + 1) * HEAD_DIM)
        q = q_ref[:, sl]
        s1 = lax.dot_general(q, kc_ref[...], dn, preferred_element_type=F32)
        m = jnp.max(s1, axis=-1, keepdims=True)
        if with_latent:
            s2 = lax.dot_general(q, kl_ref[...], dn, preferred_element_type=F32)
            m = jnp.maximum(m, jnp.max(s2, axis=-1, keepdims=True))
        p1 = jnp.exp2(s1 - m)
        l = jnp.sum(p1, axis=-1, keepdims=True)
        o = jnp.dot(p1.astype(BF16), vc_ref[...], preferred_element_type=F32)
        if with_latent:
            p2 = jnp.exp2(s2 - m)
            l = l + jnp.sum(p2, axis=-1, keepdims=True)
            o = o + jnp.dot(p2.astype(BF16), vl_ref[...], preferred_element_type=F32)
        o_ref[:, sl] = (o / l).astype(BF16)


def _attention(qk, p, *, b, s, l, n_kv, tq, out_rows=None, o_buf=None):
    kcol = n_kv * Q_PER_KV
    vcol = kcol + n_kv
    ctx0 = b * s // l
    qw = Q_PER_KV * HEAD_DIM
    latent = o_buf is None
    if latent:
        nq = s // tq
        grid = (b, n_kv, nq)
        in_specs = [pl.BlockSpec((tq, qw), lambda bi, h, qi: (bi * nq + qi, h)),
                    pl.BlockSpec((l, HEAD_DIM), lambda bi, h, qi: (ctx0 + bi, kcol + h)),
                    pl.BlockSpec((l, HEAD_DIM), lambda bi, h, qi: (ctx0 + bi, vcol + h)),
                    pl.BlockSpec((s, HEAD_DIM), lambda bi, h, qi: (bi, kcol + h)),
                    pl.BlockSpec((s, HEAD_DIM), lambda bi, h, qi: (bi, vcol + h))]
        args = (qk, qk, p, qk, p)
        out_specs = pl.BlockSpec((tq, qw), lambda bi, h, qi: (bi * nq + qi, h))
        aliases = {}
        sem = ("arbitrary", "arbitrary", "arbitrary")
    else:
        out_rows = o_buf.shape[0]
        grid = (b, n_kv)
        in_specs = [pl.BlockSpec((l, qw), lambda bi, h: (ctx0 + bi, h)),
                    pl.BlockSpec((l, HEAD_DIM), lambda bi, h: (ctx0 + bi, kcol + h)),
                    pl.BlockSpec((l, HEAD_DIM), lambda bi, h: (ctx0 + bi, vcol + h)),
                    pl.BlockSpec(memory_space=pl.ANY)]
        args = (qk, qk, p, o_buf)
        out_specs = pl.BlockSpec((l, qw), lambda bi, h: (ctx0 + bi, h))
        aliases = {3: 0}
        sem = ("arbitrary", "arbitrary")
    return pl.pallas_call(
        functools.partial(_attn_kernel, with_latent=latent),
        grid=grid,
        in_specs=in_specs,
        out_specs=out_specs,
        out_shape=jax.ShapeDtypeStruct((out_rows, n_kv * qw), BF16),
        input_output_aliases=aliases,
        compiler_params=_cparams(sem),
        name="attn_lat" if latent else "attn_ctx",
    )(*args)


def _conv_kernel(sb_ref, sc_ref, sx_ref, a_ref, gt_ref,
                 scp_ref, sxp_ref, ap_ref, gtp_ref, scn_ref, sxn_ref, an_ref, gtn_ref,
                 sw_ref, cw_ref, cb_ref, cg_ref, cbeta_ref, o_ref, z_scr, u_scr, part_scr,
                 *, ts, lat_rows, s, l, rc):
    c = sb_ref.shape[1]
    i = pl.program_id(0)
    row0 = i * ts
    is_ctx = row0 >= lat_rows
    pos = jnp.where(is_ctx, lax.rem(jnp.maximum(row0 - lat_rows, 0), l), lax.rem(row0, s))
    seqlen = jnp.where(is_ctx, l, s)
    keep_prev = (pos > 0).astype(F32)
    keep_next = (pos + ts < seqlen).astype(F32)

    def glu(a, g):
        return a.astype(F32) * jax.nn.sigmoid(g.astype(F32))

    z_scr[0:HALO, :] = scp_ref[...].astype(F32) * sxp_ref[...].astype(F32) * keep_prev
    z_scr[HALO:HALO + ts, :] = sc_ref[...].astype(F32) * sx_ref[...].astype(F32)
    z_scr[HALO + ts:2 * HALO + ts, :] = scn_ref[...].astype(F32) * sxn_ref[...].astype(F32) * keep_next
    u_scr[0:HALO, :] = glu(ap_ref[...], gtp_ref[...]) * keep_prev
    u_scr[HALO:HALO + ts, :] = glu(a_ref[...], gt_ref[...])
    u_scr[HALO + ts:2 * HALO + ts, :] = glu(an_ref[...], gtn_ref[...]) * keep_next

    spad = SCONV_WIDTH // 2
    cpad = CCONV_WIDTH // 2
    cw_tiles = [jnp.broadcast_to(cw_ref[k:k + 1, :], (SUBLANES, c)) for k in range(CCONV_WIDTH)]

    def scale_rows(v, w_tile):
        n = v.shape[0]
        return (v.reshape(n // SUBLANES, SUBLANES, c) * w_tile[None]).reshape(n, c)

    for r0 in range(0, ts, rc):
        base = HALO + r0
        acc = sw_ref[0:1, :] * z_scr[base - spad:base - spad + rc, :]
        for k in range(1, SCONV_WIDTH):
            acc = acc + sw_ref[k:k + 1, :] * z_scr[base - spad + k:base - spad + k + rc, :]
        o_ref[r0:r0 + rc, 0:c] = (sb_ref[r0:r0 + rc, :].astype(F32) * acc).astype(BF16)

        acc = None
        for res in range(SUBLANES):
            taps = [k for k in range(CCONV_WIDTH) if (HALO - cpad + k) % SUBLANES == res]
            n = rc if res == 0 else rc + SUBLANES
            part = None
            for k in taps:
                start = r0 + (HALO - cpad + k) - res
                term = scale_rows(u_scr[start:start + n, :], cw_tiles[k])
                part = term if part is None else part + term
            if res:
                part_scr[res - 1] = part
                part = part_scr[res - 1, res:res + rc, :]
            acc = part if acc is None else acc + part
        y = _ln_rows(acc + cb_ref[...]) * cg_ref[...] + cbeta_ref[...]
        o_ref[r0:r0 + rc, c:2 * c] = _silu(y).astype(BF16)


def _convs(p, li, sconv_w, cconv_w, cconv_b, cln_g, cln_b, *, rows, b, s, l, ts):
    t = p.shape[0]
    c = sconv_w.shape[-1]
    depth = sconv_w.shape[0]
    hb = ts // HALO
    nhb = t // HALO
    rc = 32
    main = lambda col: pl.BlockSpec((ts, c), lambda i: (i, col))
    prev = lambda col: pl.BlockSpec((HALO, c), lambda i: (jnp.maximum(i * hb - 1, 0), col))
    nxt = lambda col: pl.BlockSpec((HALO, c), lambda i: (jnp.minimum((i + 1) * hb, nhb - 1), col))
    vec = lambda r: pl.BlockSpec((None, r, c), lambda i: (li, 0, 0))
    in_specs = ([main(3), main(4), main(5), main(6), main(7)]
                + [prev(4), prev(5), prev(6), prev(7)] + [nxt(4), nxt(5), nxt(6), nxt(7)]
                + [vec(SCONV_WIDTH), vec(CCONV_WIDTH), vec(1), vec(1), vec(1)])
    kern = functools.partial(_conv_kernel, ts=ts, lat_rows=b * s, s=s, l=l, rc=rc)
    return pl.pallas_call(
        kern,
        grid=(rows // ts,),
        in_specs=in_specs,
        out_specs=pl.BlockSpec((ts, 2 * c), lambda i: (i, 0)),
        out_shape=jax.ShapeDtypeStruct((rows, 2 * c), BF16),
        scratch_shapes=[pltpu.VMEM((ts + 2 * HALO, c), F32), pltpu.VMEM((ts + 2 * HALO, c), F32),
                        pltpu.VMEM((SUBLANES - 1, rc + SUBLANES, c), F32)],
        compiler_params=_cparams(("arbitrary",)),
        name="convs",
    )(*([p] * 13), sconv_w, cconv_w, cconv_b.reshape(depth, 1, c), cln_g.reshape(depth, 1, c),
      cln_b.reshape(depth, 1, c))


def _rope_tables(b, s, l):
    rows = s // GRID_W
    row = jnp.repeat(jnp.arange(rows, dtype=F32), GRID_W)
    col = jnp.tile(jnp.arange(GRID_W, dtype=F32), rows)
    axis_dim = HEAD_DIM // 2
    inv_freq = ROPE_THETA ** (-jnp.arange(0, axis_dim, 2, dtype=F32) / axis_dim)
    ar, ac = row[:, None] * inv_freq, col[:, None] * inv_freq
    cos = jnp.concatenate([jnp.cos(ar), jnp.cos(ar), jnp.cos(ac), jnp.cos(ac)], -1)
    sin = jnp.concatenate([-jnp.sin(ar), jnp.sin(ar), -jnp.sin(ac), jnp.sin(ac)], -1)
    cos = jnp.concatenate([jnp.tile(cos, (b, 1)), jnp.ones((b * l, HEAD_DIM), F32)], 0)
    sin = jnp.concatenate([jnp.tile(sin, (b, 1)), jnp.zeros((b * l, HEAD_DIM), F32)], 0)
    return cos, sin


def kernel(x, c, ctx, c_ctx, w_ada, b_ada, post_ln_g, post_ln_b, ffn_w_gu, ffn_w_down, w_in, w_out, q_norm_g,
           k_norm_g, sconv_w, cconv_w, cconv_b, cconv_ln_g, cconv_ln_b):
    b, s, d = x.shape
    l = ctx.shape[1]
    depth = w_ada.shape[0]
    n_q = d // (2 * HEAD_DIM)
    n_kv = n_q // Q_PER_KV
    alpha = (2.0 * depth) ** 0.25
    q_scale = HEAD_DIM ** -0.5 * math.log2(math.e)
    lat = b * s
    t = lat + b * l
    tm = min(1024, s, b * l)
    tmr = min(512, tm)
    ts = min(256, l)
    tk = 512
    assert s % tm == 0 and (b * l) % tm == 0 and s % ts == 0 and l % ts == 0 and s % l == 0
    assert ts % HALO == 0 and HALO > CCONV_WIDTH // 2

    def group_of(i, tile):
        return jnp.minimum(i * tile // s, b)

    xs = jnp.concatenate([x.reshape(lat, d), ctx.reshape(b * l, d)], 0)
    cvec = jnp.zeros((8, d), F32).at[:b].set(c).at[b].set(c_ctx)
    mods = _adaln(cvec, w_ada, b_ada).reshape(depth, 8, N_MOD, d)
    cos, sin = _rope_tables(b, s, l)
    gains = jnp.concatenate([jnp.tile(q_norm_g * q_scale, (1, n_q)), jnp.tile(k_norm_g, (1, n_kv))], 1)[:, None, :]
    w_gu16, w_dn16 = ffn_w_gu.astype(BF16), ffn_w_down.astype(BF16)
    w_in16, w_out16 = w_in.astype(BF16), w_out.astype(BF16)
    res = functools.partial(_mm_res, mods=mods, post_g=post_ln_g, post_b=post_ln_b, group_of=group_of, alpha=alpha,
                            tm=tmr, tk=tk)

    h = _lnmod(xs, mods, 0, group_of, 0, tmr)
    for li in range(depth):
        last = li == depth - 1
        rows = lat if last else t

        a = _mm_glu(h, w_gu16, (li, 0), t, tm, 512)
        xs, h = res(a, w_dn16, (li, 0), xs, li=li, lin=li, rows=t, gate_row=2, gate_mul=0.5, ln_row=0, next_row=3)

        p = _mm_plain(h, w_in16, (li,), tm, 512)
        qk = _qk_prep(p, gains, li, cos, sin, tm)
        y_attn = _attention(qk, p, b=b, s=s, l=l, n_kv=n_kv, tq=min(256, s), out_rows=rows)
        if not last:
            y_attn = _attention(qk, p, b=b, s=s, l=l, n_kv=n_kv, tq=l, o_buf=y_attn)
        y_conv = _convs(p, li, sconv_w, cconv_w, cconv_b, cconv_ln_g, cconv_ln_b, rows=rows, b=b, s=s, l=l, ts=ts)
        xs, h = res(y_attn, w_out16, (li,), xs, li=li, lin=li, rows=rows, gate_row=5, gate_mul=1.0, ln_row=1,
                    next_row=6, a2=y_conv)

        a = _mm_glu(h, w_gu16, (li, 1), rows, tm, 512)
        xs, h = res(a, w_dn16, (li, 1), xs, li=li, lin=li + 1, rows=rows, gate_row=8, gate_mul=0.5, ln_row=2,
                    next_row=None if last else 0)
    return xs.reshape(b, s, d)
```

```python
import functools
import math

import jax
import jax.numpy as jnp
from jax import lax
from jax.experimental import pallas as pl
from jax.experimental.pallas import tpu as pltpu

F32 = jnp.float32
BF16 = jnp.bfloat16

HEAD_DIM = 128
Q_PER_KV = 4
GRID_W = 64
ROPE_THETA = 10000.0
SCONV_WIDTH = 3
CCONV_WIDTH = 31
N_MOD = 9
EPS = 1e-6
SUBLANES = 8
EPI_ROWS = 8
HALO = 16
V7X_VMEM_LIMIT = 60000 * 1024


def _cparams(sem):
    return pltpu.CompilerParams(dimension_semantics=sem, vmem_limit_bytes=V7X_VMEM_LIMIT)


def _silu(v):
    return v * jax.nn.sigmoid(v)


def _ln_rows(z):
    mu = jnp.mean(z, axis=-1, keepdims=True)
    zc = z - mu
    var = jnp.mean(zc * zc, axis=-1, keepdims=True)
    return zc * lax.rsqrt(var + EPS)


def _stacked(block, lead, index):
    return pl.BlockSpec((None,) * len(lead) + block, lambda *g: lead + index(*g))


def _adaln_kernel(c_ref, w_ref, b_ref, o_ref):
    sc = _silu(c_ref[...]).astype(BF16)
    o_ref[...] = jnp.dot(sc, w_ref[...].astype(BF16), preferred_element_type=F32) + b_ref[...]


def _adaln(cvec, w_ada, b_ada, tn=512):
    depth, d, n = w_ada.shape
    return pl.pallas_call(
        _adaln_kernel,
        grid=(depth, n // tn),
        in_specs=[pl.BlockSpec((8, d), lambda l, j: (0, 0)),
                  pl.BlockSpec((None, d, tn), lambda l, j: (l, 0, j)),
                  pl.BlockSpec((None, 1, tn), lambda l, j: (l, 0, j))],
        out_specs=pl.BlockSpec((None, 8, tn), lambda l, j: (l, 0, j)),
        out_shape=jax.ShapeDtypeStruct((depth, 8, n), F32),
        compiler_params=_cparams(("arbitrary", "arbitrary")),
        name="adaln",
    )(cvec, w_ada, b_ada.reshape(depth, 1, n))


def _lnmod_kernel(x_ref, mod_ref, h_ref, *, shift_row, rows):
    shift = mod_ref[shift_row:shift_row + 1, :]
    scale1 = 1.0 + mod_ref[shift_row + 1:shift_row + 2, :]

    def body(r, carry):
        sl = pl.ds(pl.multiple_of(r * rows, rows), rows)
        h_ref[sl, :] = (_ln_rows(x_ref[sl, :]) * scale1 + shift).astype(BF16)
        return carry

    lax.fori_loop(0, x_ref.shape[0] // rows, body, 0)


def _lnmod(x, mods, li, group_of, shift_row, tm):
    t, d = x.shape
    return pl.pallas_call(
        functools.partial(_lnmod_kernel, shift_row=shift_row, rows=64),
        grid=(t // tm,),
        in_specs=[pl.BlockSpec((tm, d), lambda i: (i, 0)),
                  pl.BlockSpec((None, None, N_MOD, d), lambda i: (li, group_of(i, tm), 0, 0))],
        out_specs=pl.BlockSpec((tm, d), lambda i: (i, 0)),
        out_shape=jax.ShapeDtypeStruct((t, d), BF16),
        compiler_params=_cparams(("arbitrary",)),
        name="lnmod",
    )(x, mods)


def _glu_kernel(h_ref, wg_ref, wu_ref, o_ref):
    h = h_ref[...]
    g = jnp.dot(h, wg_ref[...], preferred_element_type=F32)
    u = jnp.dot(h, wu_ref[...], preferred_element_type=F32)
    o_ref[...] = (_silu(g) * u).astype(BF16)


def _mm_glu(h, w_gu, lead, rows, tm, tn):
    d = h.shape[1]
    f = w_gu.shape[-1] // 2
    nj = f // tn
    return pl.pallas_call(
        _glu_kernel,
        grid=(rows // tm, nj),
        in_specs=[pl.BlockSpec((tm, d), lambda i, j: (i, 0)),
                  _stacked((d, tn), lead, lambda i, j: (0, j)),
                  _stacked((d, tn), lead, lambda i, j: (0, j + nj))],
        out_specs=pl.BlockSpec((tm, tn), lambda i, j: (i, j)),
        out_shape=jax.ShapeDtypeStruct((rows, f), BF16),
        compiler_params=_cparams(("arbitrary", "arbitrary")),
        name="mm_glu",
    )(h, w_gu, w_gu)


def _plain_kernel(h_ref, w_ref, o_ref):
    o_ref[...] = jnp.dot(h_ref[...], w_ref[...], preferred_element_type=F32).astype(BF16)


def _mm_plain(h, w, lead, tm, tn):
    t, d = h.shape
    n = w.shape[-1]
    return pl.pallas_call(
        _plain_kernel,
        grid=(t // tm, n // tn),
        in_specs=[pl.BlockSpec((tm, d), lambda i, j: (i, 0)),
                  _stacked((d, tn), lead, lambda i, j: (0, j))],
        out_specs=pl.BlockSpec((tm, tn), lambda i, j: (i, j)),
        out_shape=jax.ShapeDtypeStruct((t, n), BF16),
        compiler_params=_cparams(("arbitrary", "arbitrary")),
        name="mm_in",
    )(h, w)


def _mmres_kernel(*refs, nt, nk, gate_row, gate_mul, alpha, ln_row, next_row, cn, two_a):
    refs = list(refs)
    a_ref = refs.pop(0)
    a2_ref = refs.pop(0) if two_a else None
    w_ref, x_ref, mod_ref = refs[:3]
    refs = refs[3:]
    modn_ref = refs.pop(0) if next_row is not None else None
    lng_ref, lnb_ref, xo_ref = refs[:3]
    refs = refs[3:]
    ho_ref = refs.pop(0) if next_row is not None else None
    acc_even, acc_odd = refs
    i = pl.program_id(0)
    k = pl.program_id(1)
    rc, n = x_ref.shape

    @pl.when((i == 0) & (k == 0))
    def _():
        acc_even[...] = jnp.zeros_like(acc_even)
        acc_odd[...] = jnp.zeros_like(acc_odd)

    def matmul(acc):
        a = jnp.where(k < nk // 2, a_ref[...], a2_ref[...]) if two_a else a_ref[...]
        for c in range(n // cn):
            sl = slice(c * cn, (c + 1) * cn)
            acc[:, sl] += jnp.dot(a, w_ref[:, sl], preferred_element_type=F32)

    def finish_rows(acc):
        gate = mod_ref[gate_row:gate_row + 1, :] * gate_mul
        g = lng_ref[ln_row:ln_row + 1, :]
        b = lnb_ref[ln_row:ln_row + 1, :]
        if next_row is not None:
            shift = modn_ref[next_row:next_row + 1, :]
            scale1 = 1.0 + modn_ref[next_row + 1:next_row + 2, :]
        for sub in range(rc // EPI_ROWS):
            blk = slice(sub * EPI_ROWS, (sub + 1) * EPI_ROWS)
            rs = pl.ds(pl.multiple_of(k * rc + sub * EPI_ROWS, EPI_ROWS), EPI_ROWS)
            y = _ln_rows(alpha * x_ref[blk, :] + gate * acc[rs, :]) * g + b
            acc[rs, :] = jnp.zeros((EPI_ROWS, n), F32)
            xo_ref[blk, :] = y
            if next_row is not None:
                ho_ref[blk, :] = (_ln_rows(y) * scale1 + shift).astype(BF16)

    for parity, (cur, prev) in enumerate(((acc_even, acc_odd), (acc_odd, acc_even))):
        @pl.when((i < nt) & (lax.rem(i, 2) == parity))
        def _(cur=cur, prev=prev):
            matmul(cur)
            finish_rows(prev)

    @pl.when(i == nt)
    def _():
        finish_rows(acc_even if (nt - 1) % 2 == 0 else acc_odd)


def _mm_res(a, w, wlead, x, mods, li, lin, post_g, post_b, group_of, *, rows, gate_row, gate_mul, alpha, ln_row,
            next_row, tm, tk, a2=None):
    d = x.shape[1]
    kdim = w.shape[-2]
    nk = kdim // tk
    nt = rows // tm
    rc = tm // nk
    assert tm % nk == 0 and rc % EPI_ROWS == 0
    two_a = a2 is not None
    tile = lambda i: jnp.minimum(i, nt - 1)
    kch = lambda i, k: jnp.where(i < nt, k, nk - 1)
    prev_chunk = lambda i, k: jnp.where(i == 0, 0, (i - 1) * nk + k)
    if two_a:
        half = nk // 2
        a_specs = [pl.BlockSpec((tm, tk), lambda i, k: (tile(i), jnp.minimum(kch(i, k), half - 1))),
                   pl.BlockSpec((tm, tk), lambda i, k: (tile(i), jnp.maximum(kch(i, k) - half, 0)))]
        a_args = [a, a2]
    else:
        a_specs = [pl.BlockSpec((tm, tk), lambda i, k: (tile(i), kch(i, k)))]
        a_args = [a]
    mod_spec = lambda layer: pl.BlockSpec(
        (None, None, N_MOD, d), lambda i, k: (layer, group_of(jnp.maximum(i - 1, 0), tm), 0, 0))
    ln_spec = pl.BlockSpec((None, 3, d), lambda i, k: (li, 0, 0))
    x_spec = pl.BlockSpec((rc, d), lambda i, k: (jnp.maximum(i - 1, 0) * nk + k, 0))
    out_spec = pl.BlockSpec((rc, d), lambda i, k: (prev_chunk(i, k), 0))
    in_specs = a_specs + [_stacked((tk, d), wlead, lambda i, k: (kch(i, k), 0)), x_spec, mod_spec(li)]
    args = a_args + [w, x, mods]
    if next_row is not None:
        in_specs.append(mod_spec(lin))
        args.append(mods)
        out_specs = [out_spec, out_spec]
        out_shape = [jax.ShapeDtypeStruct((rows, d), F32), jax.ShapeDtypeStruct((rows, d), BF16)]
    else:
        out_specs = [out_spec]
        out_shape = [jax.ShapeDtypeStruct((rows, d), F32)]
    in_specs += [ln_spec, ln_spec]
    args += [post_g, post_b]
    kern = functools.partial(_mmres_kernel, nt=nt, nk=nk, gate_row=gate_row, gate_mul=gate_mul, alpha=alpha,
                             ln_row=ln_row, next_row=next_row, cn=min(1024, d), two_a=two_a)
    outs = pl.pallas_call(
        kern,
        grid=(nt + 1, nk),
        in_specs=in_specs,
        out_specs=out_specs,
        out_shape=out_shape,
        scratch_shapes=[pltpu.VMEM((tm, d), F32), pltpu.VMEM((tm, d), F32)],
        compiler_params=_cparams(("arbitrary", "arbitrary")),
        name="mm_res",
    )(*args)
    return (outs[0], outs[1]) if next_row is not None else (outs[0], None)


def _qkprep_kernel(p_ref, g_ref, cos_ref, sin_ref, o_ref):
    cos = cos_ref[...]
    sin = sin_ref[...]
    lane = lax.broadcasted_iota(jnp.int32, cos.shape, 1)
    lower = (lane & (HEAD_DIM // 4)) == 0
    for hd in range(p_ref.shape[1] // HEAD_DIM):
        sl = slice(hd * HEAD_DIM, (hd + 1) * HEAD_DIM)
        v = p_ref[:, sl].astype(F32)
        y = v * lax.rsqrt(jnp.mean(v * v, axis=-1, keepdims=True) + EPS) * g_ref[:, sl]
        partner = jnp.where(lower, pltpu.roll(y, HEAD_DIM - HEAD_DIM // 4, 1), pltpu.roll(y, HEAD_DIM // 4, 1))
        o_ref[:, sl] = (y * cos + partner * sin).astype(BF16)


def _qk_prep(p, gains, li, cos, sin, tm):
    t = p.shape[0]
    nqk = gains.shape[-1]
    heads = nqk // HEAD_DIM
    tn = HEAD_DIM * max(n for n in (1, 2, 4) if heads % n == 0)
    return pl.pallas_call(
        _qkprep_kernel,
        grid=(t // tm, nqk // tn),
        in_specs=[pl.BlockSpec((tm, tn), lambda i, j: (i, j)),
                  pl.BlockSpec((None, 1, tn), lambda i, j: (li, 0, j)),
                  pl.BlockSpec((tm, HEAD_DIM), lambda i, j: (i, 0)),
                  pl.BlockSpec((tm, HEAD_DIM), lambda i, j: (i, 0))],
        out_specs=pl.BlockSpec((tm, tn), lambda i, j: (i, j)),
        out_shape=jax.ShapeDtypeStruct((t, nqk), BF16),
        compiler_params=_cparams(("arbitrary", "arbitrary")),
        name="qk_prep",
    )(p, gains, cos, sin)


def _attn_kernel(*refs, with_latent):
    if with_latent:
        q_ref, kc_ref, vc_ref, kl_ref, vl_ref, o_ref = refs
    else:
        q_ref, kc_ref, vc_ref, _, o_ref = refs
    dn = (((1,), (1,)), ((), ()))
    for g in range(Q_PER_KV):
        sl = slice(g * HEAD_DIM, (g + 1) * HEAD_DIM)
        q = q_ref[:, sl]
        s1 = lax.dot_general(q, kc_ref[...], dn, preferred_element_type=F32)
        m = jnp.max(s1, axis=-1, keepdims=True)
        if with_latent:
            s2 = lax.dot_general(q, kl_ref[...], dn, preferred_element_type=F32)
            m = jnp.maximum(m, jnp.max(s2, axis=-1, keepdims=True))
        p1 = jnp.exp2(s1 - m)
        l = jnp.sum(p1, axis=-1, keepdims=True)
        o = jnp.dot(p1.astype(BF16), vc_ref[...], preferred_element_type=F32)
        if with_latent:
            p2 = jnp.exp2(s2 - m)
            l = l + jnp.sum(p2, axis=-1, keepdims=True)
            o = o + jnp.dot(p2.astype(BF16), vl_ref[...], preferred_element_type=F32)
        o_ref[:, sl] = (o / l).astype(BF16)


def _attention(qk, p, *, b, s, l, n_kv, tq, out_rows=None, o_buf=None):
    kcol = n_kv * Q_PER_KV
    vcol = kcol + n_kv
    ctx0 = b * s // l
    qw = Q_PER_KV * HEAD_DIM
    latent = o_buf is None
    if latent:
        nq = s // tq
        grid = (b, n_kv, nq)
        in_specs = [pl.BlockSpec((tq, qw), lambda bi, h, qi: (bi * nq + qi, h)),
                    pl.BlockSpec((l, HEAD_DIM), lambda bi, h, qi: (ctx0 + bi, kcol + h)),
                    pl.BlockSpec((l, HEAD_DIM), lambda bi, h, qi: (ctx0 + bi, vcol + h)),
                    pl.BlockSpec((s, HEAD_DIM), lambda bi, h, qi: (bi, kcol + h)),
                    pl.BlockSpec((s, HEAD_DIM), lambda bi, h, qi: (bi, vcol + h))]
        args = (qk, qk, p, qk, p)
        out_specs = pl.BlockSpec((tq, qw), lambda bi, h, qi: (bi * nq + qi, h))
        aliases = {}
        sem = ("arbitrary", "arbitrary", "arbitrary")
    else:
        out_rows = o_buf.shape[0]
        grid = (b, n_kv)
        in_specs = [pl.BlockSpec((l, qw), lambda bi, h: (ctx0 + bi, h)),
                    pl.BlockSpec((l, HEAD_DIM), lambda bi, h: (ctx0 + bi, kcol + h)),
                    pl.BlockSpec((l, HEAD_DIM), lambda bi, h: (ctx0 + bi, vcol + h)),
                    pl.BlockSpec(memory_space=pl.ANY)]
        args = (qk, qk, p, o_buf)
        out_specs = pl.BlockSpec((l, qw), lambda bi, h: (ctx0 + bi, h))
        aliases = {3: 0}
        sem = ("arbitrary", "arbitrary")
    return pl.pallas_call(
        functools.partial(_attn_kernel, with_latent=latent),
        grid=grid,
        in_specs=in_specs,
        out_specs=out_specs,
        out_shape=jax.ShapeDtypeStruct((out_rows, n_kv * qw), BF16),
        input_output_aliases=aliases,
        compiler_params=_cparams(sem),
        name="attn_lat" if latent else "attn_ctx",
    )(*args)


def _conv_kernel(sb_ref, sc_ref, sx_ref, a_ref, gt_ref,
                 scp_ref, sxp_ref, ap_ref, gtp_ref, scn_ref, sxn_ref, an_ref, gtn_ref,
                 sw_ref, cw_ref, cb_ref, cg_ref, cbeta_ref, o_ref, z_scr, u_scr, part_scr,
                 *, ts, lat_rows, s, l, rc):
    c = sb_ref.shape[1]
    i = pl.program_id(0)
    row0 = i * ts
    is_ctx = row0 >= lat_rows
    pos = jnp.where(is_ctx, lax.rem(jnp.maximum(row0 - lat_rows, 0), l), lax.rem(row0, s))
    seqlen = jnp.where(is_ctx, l, s)
    keep_prev = (pos > 0).astype(F32)
    keep_next = (pos + ts < seqlen).astype(F32)

    def glu(a, g):
        return a.astype(F32) * jax.nn.sigmoid(g.astype(F32))

    z_scr[0:HALO, :] = scp_ref[...].astype(F32) * sxp_ref[...].astype(F32) * keep_prev
    z_scr[HALO:HALO + ts, :] = sc_ref[...].astype(F32) * sx_ref[...].astype(F32)
    z_scr[HALO + ts:2 * HALO + ts, :] = scn_ref[...].astype(F32) * sxn_ref[...].astype(F32) * keep_next
    u_scr[0:HALO, :] = glu(ap_ref[...], gtp_ref[...]) * keep_prev
    u_scr[HALO:HALO + ts, :] = glu(a_ref[...], gt_ref[...])
    u_scr[HALO + ts:2 * HALO + ts, :] = glu(an_ref[...], gtn_ref[...]) * keep_next

    spad = SCONV_WIDTH // 2
    cpad = CCONV_WIDTH // 2
    cw_tiles = [jnp.broadcast_to(cw_ref[k:k + 1, :], (SUBLANES, c)) for k in range(CCONV_WIDTH)]

    def scale_rows(v, w_tile):
        n = v.shape[0]
        return (v.reshape(n // SUBLANES, SUBLANES, c) * w_tile[None]).reshape(n, c)

    for r0 in range(0, ts, rc):
        base = HALO + r0
        acc = sw_ref[0:1, :] * z_scr[base - spad:base - spad + rc, :]
        for k in range(1, SCONV_WIDTH):
            acc = acc + sw_ref[k:k + 1, :] * z_scr[base - spad + k:base - spad + k + rc, :]
        o_ref[r0:r0 + rc, 0:c] = (sb_ref[r0:r0 + rc, :].astype(F32) * acc).astype(BF16)

        acc = None
        for res in range(SUBLANES):
            taps = [k for k in range(CCONV_WIDTH) if (HALO - cpad + k) % SUBLANES == res]
            n = rc if res == 0 else rc + SUBLANES
            part = None
            for k in taps:
                start = r0 + (HALO - cpad + k) - res
                term = scale_rows(u_scr[start:start + n, :], cw_tiles[k])
                part = term if part is None else part + term
            if res:
                part_scr[res - 1] = part
                part = part_scr[res - 1, res:res + rc, :]
            acc = part if acc is None else acc + part
        y = _ln_rows(acc + cb_ref[...]) * cg_ref[...] + cbeta_ref[...]
        o_ref[r0:r0 + rc, c:2 * c] = _silu(y).astype(BF16)


def _convs(p, li, sconv_w, cconv_w, cconv_b, cln_g, cln_b, *, rows, b, s, l, ts):
    t = p.shape[0]
    c = sconv_w.shape[-1]
    depth = sconv_w.shape[0]
    hb = ts // HALO
    nhb = t // HALO
    rc = 32
    main = lambda col: pl.BlockSpec((ts, c), lambda i: (i, col))
    prev = lambda col: pl.BlockSpec((HALO, c), lambda i: (jnp.maximum(i * hb - 1, 0), col))
    nxt = lambda col: pl.BlockSpec((HALO, c), lambda i: (jnp.minimum((i + 1) * hb, nhb - 1), col))
    vec = lambda r: pl.BlockSpec((None, r, c), lambda i: (li, 0, 0))
    in_specs = ([main(3), main(4), main(5), main(6), main(7)]
                + [prev(4), prev(5), prev(6), prev(7)] + [nxt(4), nxt(5), nxt(6), nxt(7)]
                + [vec(SCONV_WIDTH), vec(CCONV_WIDTH), vec(1), vec(1), vec(1)])
    kern = functools.partial(_conv_kernel, ts=ts, lat_rows=b * s, s=s, l=l, rc=rc)
    return pl.pallas_call(
        kern,
        grid=(rows // ts,),
        in_specs=in_specs,
        out_specs=pl.BlockSpec((ts, 2 * c), lambda i: (i, 0)),
        out_shape=jax.ShapeDtypeStruct((rows, 2 * c), BF16),
        scratch_shapes=[pltpu.VMEM((ts + 2 * HALO, c), F32), pltpu.VMEM((ts + 2 * HALO, c), F32),
                        pltpu.VMEM((SUBLANES - 1, rc + SUBLANES, c), F32)],
        compiler_params=_cparams(("arbitrary",)),
        name="convs",
    )(*([p] * 13), sconv_w, cconv_w, cconv_b.reshape(depth, 1, c), cln_g.reshape(depth, 1, c),
      cln_b.reshape(depth, 1, c))


def _rope_tables(b, s, l):
    rows = s // GRID_W
    row = jnp.repeat(jnp.arange(rows, dtype=F32), GRID_W)
    col = jnp.tile(jnp.arange(GRID_W, dtype=F32), rows)
    axis_dim = HEAD_DIM // 2
    inv_freq = ROPE_THETA ** (-jnp.arange(0, axis_dim, 2, dtype=F32) / axis_dim)
    ar, ac = row[:, None] * inv_freq, col[:, None] * inv_freq
    cos = jnp.concatenate([jnp.cos(ar), jnp.cos(ar), jnp.cos(ac), jnp.cos(ac)], -1)
    sin = jnp.concatenate([-jnp.sin(ar), jnp.sin(ar), -jnp.sin(ac), jnp.sin(ac)], -1)
    cos = jnp.concatenate([jnp.tile(cos, (b, 1)), jnp.ones((b * l, HEAD_DIM), F32)], 0)
    sin = jnp.concatenate([jnp.tile(sin, (b, 1)), jnp.zeros((b * l, HEAD_DIM), F32)], 0)
    return cos, sin


def kernel(x, c, ctx, c_ctx, w_ada, b_ada, post_ln_g, post_ln_b, ffn_w_gu, ffn_w_down, w_in, w_out, q_norm_g,
           k_norm_g, sconv_w, cconv_w, cconv_b, cconv_ln_g, cconv_ln_b):
    b, s, d = x.shape
    l = ctx.shape[1]
    depth = w_ada.shape[0]
    n_q = d // (2 * HEAD_DIM)
    n_kv = n_q // Q_PER_KV
    alpha = (2.0 * depth) ** 0.25
    q_scale = HEAD_DIM ** -0.5 * math.log2(math.e)
    lat = b * s
    t = lat + b * l
    tm = min(1024, s, b * l)
    ts = min(256, l)
    tmr = tm
    tk = 512
    assert s % tm == 0 and (b * l) % tm == 0 and s % ts == 0 and l % ts == 0 and s % l == 0
    assert ts % HALO == 0 and HALO > CCONV_WIDTH // 2

    def group_of(i, tile):
        return jnp.minimum(i * tile // s, b)

    xs = jnp.concatenate([x.reshape(lat, d), ctx.reshape(b * l, d)], 0)
    cvec = jnp.zeros((8, d), F32).at[:b].set(c).at[b].set(c_ctx)
    mods = _adaln(cvec, w_ada, b_ada).reshape(depth, 8, N_MOD, d)
    cos, sin = _rope_tables(b, s, l)
    gains = jnp.concatenate([jnp.tile(q_norm_g * q_scale, (1, n_q)), jnp.tile(k_norm_g, (1, n_kv))], 1)[:, None, :]
    w_gu16, w_dn16 = ffn_w_gu.astype(BF16), ffn_w_down.astype(BF16)
    w_in16, w_out16 = w_in.astype(BF16), w_out.astype(BF16)
    res = functools.partial(_mm_res, mods=mods, post_g=post_ln_g, post_b=post_ln_b, group_of=group_of, alpha=alpha,
                            tm=tmr, tk=tk)

    h = _lnmod(xs, mods, 0, group_of, 0, min(512, tm))
    for li in range(depth):
        last = li == depth - 1
        rows = lat if last else t

        a = _mm_glu(h, w_gu16, (li, 0), t, tm, 512)
        xs, h = res(a, w_dn16, (li, 0), xs, li=li, lin=li, rows=t, gate_row=2, gate_mul=0.5, ln_row=0, next_row=3)

        p = _mm_plain(h, w_in16, (li,), tm, 512)
        qk = _qk_prep(p, gains, li, cos, sin, tm)
        y_attn = _attention(qk, p, b=b, s=s, l=l, n_kv=n_kv, tq=min(256, s), out_rows=rows)
        if not last:
            y_attn = _attention(qk, p, b=b, s=s, l=l, n_kv=n_kv, tq=l, o_buf=y_attn)
        y_conv = _convs(p, li, sconv_w, cconv_w, cconv_b, cconv_ln_g, cconv_ln_b, rows=rows, b=b, s=s, l=l, ts=ts)
        xs, h = res(y_attn, w_out16, (li,), xs, li=li, lin=li, rows=rows, gate_row=5, gate_mul=1.0, ln_row=1,
                    next_row=6, a2=y_conv)

        a = _mm_glu(h, w_gu16, (li, 1), rows, tm, 512)
        xs, h = res(a, w_dn16, (li, 1), xs, li=li, lin=li + 1, rows=rows, gate_row=8, gate_mul=0.5, ln_row=2,
                    next_row=None if last else 0)
    return xs.reshape(b, s, d)
```

```python
import functools
import math

import jax
import jax.numpy as jnp
from jax import lax
from jax.experimental import pallas as pl
from jax.experimental.pallas import tpu as pltpu

F32 = jnp.float32
BF16 = jnp.bfloat16

HEAD_DIM = 128
Q_PER_KV = 4
GRID_W = 64
ROPE_THETA = 10000.0
SCONV_WIDTH = 3
CCONV_WIDTH = 31
N_MOD = 9
EPS = 1e-6
SUBLANES = 8
EPI_ROWS = 8
HALO = 16
V7X_VMEM_LIMIT = 60000 * 1024


def _cparams(sem):
    return pltpu.CompilerParams(dimension_semantics=sem, vmem_limit_bytes=V7X_VMEM_LIMIT)


def _silu(v):
    return v * jax.nn.sigmoid(v)


def _ln_rows(z):
    mu = jnp.mean(z, axis=-1, keepdims=True)
    zc = z - mu
    var = jnp.mean(zc * zc, axis=-1, keepdims=True)
    return zc * lax.rsqrt(var + EPS)


def _stacked(block, lead, index):
    return pl.BlockSpec((None,) * len(lead) + block, lambda *g: lead + index(*g))


def _adaln_kernel(c_ref, w_ref, b_ref, o_ref):
    sc = _silu(c_ref[...]).astype(BF16)
    o_ref[...] = jnp.dot(sc, w_ref[...].astype(BF16), preferred_element_type=F32) + b_ref[...]


def _adaln(cvec, w_ada, b_ada, tn=512):
    depth, d, n = w_ada.shape
    return pl.pallas_call(
        _adaln_kernel,
        grid=(depth, n // tn),
        in_specs=[pl.BlockSpec((8, d), lambda l, j: (0, 0)),
                  pl.BlockSpec((None, d, tn), lambda l, j: (l, 0, j)),
                  pl.BlockSpec((None, 1, tn), lambda l, j: (l, 0, j))],
        out_specs=pl.BlockSpec((None, 8, tn), lambda l, j: (l, 0, j)),
        out_shape=jax.ShapeDtypeStruct((depth, 8, n), F32),
        compiler_params=_cparams(("arbitrary", "arbitrary")),
        name="adaln",
    )(cvec, w_ada, b_ada.reshape(depth, 1, n))


def _lnmod_kernel(x_ref, mod_ref, h_ref, *, shift_row, rows):
    shift = mod_ref[shift_row:shift_row + 1, :]
    scale1 = 1.0 + mod_ref[shift_row + 1:shift_row + 2, :]

    def body(r, carry):
        sl = pl.ds(pl.multiple_of(r * rows, rows), rows)
        h_ref[sl, :] = (_ln_rows(x_ref[sl, :]) * scale1 + shift).astype(BF16)
        return carry

    lax.fori_loop(0, x_ref.shape[0] // rows, body, 0)


def _lnmod(x, mods, li, group_of, shift_row, tm):
    t, d = x.shape
    return pl.pallas_call(
        functools.partial(_lnmod_kernel, shift_row=shift_row, rows=64),
        grid=(t // tm,),
        in_specs=[pl.BlockSpec((tm, d), lambda i: (i, 0)),
                  pl.BlockSpec((None, None, N_MOD, d), lambda i: (li, group_of(i, tm), 0, 0))],
        out_specs=pl.BlockSpec((tm, d), lambda i: (i, 0)),
        out_shape=jax.ShapeDtypeStruct((t, d), BF16),
        compiler_params=_cparams(("arbitrary",)),
        name="lnmod",
    )(x, mods)


def _glu_kernel(h_ref, wg_ref, wu_ref, o_ref):
    h = h_ref[...]
    g = jnp.dot(h, wg_ref[...], preferred_element_type=F32)
    u = jnp.dot(h, wu_ref[...], preferred_element_type=F32)
    o_ref[...] = (_silu(g) * u).astype(BF16)


def _mm_glu(h, w_gu, lead, rows, tm, tn):
    d = h.shape[1]
    f = w_gu.shape[-1] // 2
    nj = f // tn
    return pl.pallas_call(
        _glu_kernel,
        grid=(rows // tm, nj),
        in_specs=[pl.BlockSpec((tm, d), lambda i, j: (i, 0)),
                  _stacked((d, tn), lead, lambda i, j: (0, j)),
                  _stacked((d, tn), lead, lambda i, j: (0, j + nj))],
        out_specs=pl.BlockSpec((tm, tn), lambda i, j: (i, j)),
        out_shape=jax.ShapeDtypeStruct((rows, f), BF16),
        compiler_params=_cparams(("arbitrary", "arbitrary")),
        name="mm_glu",
    )(h, w_gu, w_gu)


def _plain_kernel(h_ref, w_ref, o_ref):
    o_ref[...] = jnp.dot(h_ref[...], w_ref[...], preferred_element_type=F32).astype(BF16)


def _mm_plain(h, w, lead, tm, tn):
    t, d = h.shape
    n = w.shape[-1]
    return pl.pallas_call(
        _plain_kernel,
        grid=(t // tm, n // tn),
        in_specs=[pl.BlockSpec((tm, d), lambda i, j: (i, 0)),
                  _stacked((d, tn), lead, lambda i, j: (0, j))],
        out_specs=pl.BlockSpec((tm, tn), lambda i, j: (i, j)),
        out_shape=jax.ShapeDtypeStruct((t, n), BF16),
        compiler_params=_cparams(("arbitrary", "arbitrary")),
        name="mm_in",
    )(h, w)


def _mmres_kernel(*refs, nt, nk, gate_row, gate_mul, alpha, ln_row, next_row, cn, two_a):
    refs = list(refs)
    a_ref = refs.pop(0)
    a2_ref = refs.pop(0) if two_a else None
    w_ref, x_ref, mod_ref = refs[:3]
    refs = refs[3:]
    modn_ref = refs.pop(0) if next_row is not None else None
    lng_ref, lnb_ref, xo_ref = refs[:3]
    refs = refs[3:]
    ho_ref = refs.pop(0) if next_row is not None else None
    acc_even, acc_odd = refs
    i = pl.program_id(0)
    k = pl.program_id(1)
    rc, n = x_ref.shape

    @pl.when((i == 0) & (k == 0))
    def _():
        acc_even[...] = jnp.zeros_like(acc_even)
        acc_odd[...] = jnp.zeros_like(acc_odd)

    def matmul(acc):
        a = jnp.where(k < nk // 2, a_ref[...], a2_ref[...]) if two_a else a_ref[...]
        for c in range(n // cn):
            sl = slice(c * cn, (c + 1) * cn)
            acc[:, sl] += jnp.dot(a, w_ref[:, sl], preferred_element_type=F32)

    def finish_rows(acc):
        gate = mod_ref[gate_row:gate_row + 1, :] * gate_mul
        g = lng_ref[ln_row:ln_row + 1, :]
        b = lnb_ref[ln_row:ln_row + 1, :]
        if next_row is not None:
            shift = modn_ref[next_row:next_row + 1, :]
            scale1 = 1.0 + modn_ref[next_row + 1:next_row + 2, :]
        for sub in range(rc // EPI_ROWS):
            blk = slice(sub * EPI_ROWS, (sub + 1) * EPI_ROWS)
            rs = pl.ds(pl.multiple_of(k * rc + sub * EPI_ROWS, EPI_ROWS), EPI_ROWS)
            y = _ln_rows(alpha * x_ref[blk, :] + gate * acc[rs, :]) * g + b
            acc[rs, :] = jnp.zeros((EPI_ROWS, n), F32)
            xo_ref[blk, :] = y
            if next_row is not None:
                ho_ref[blk, :] = (_ln_rows(y) * scale1 + shift).astype(BF16)

    for parity, (cur, prev) in enumerate(((acc_even, acc_odd), (acc_odd, acc_even))):
        @pl.when((i < nt) & (lax.rem(i, 2) == parity))
        def _(cur=cur, prev=prev):
            matmul(cur)
            finish_rows(prev)

    @pl.when(i == nt)
    def _():
        finish_rows(acc_even if (nt - 1) % 2 == 0 else acc_odd)


def _mm_res(a, w, wlead, x, mods, li, lin, post_g, post_b, group_of, *, rows, gate_row, gate_mul, alpha, ln_row,
            next_row, tm, tk, a2=None):
    d = x.shape[1]
    kdim = w.shape[-2]
    nk = kdim // tk
    nt = rows // tm
    rc = tm // nk
    assert tm % nk == 0 and rc % EPI_ROWS == 0
    two_a = a2 is not None
    tile = lambda i: jnp.minimum(i, nt - 1)
    kch = lambda i, k: jnp.where(i < nt, k, nk - 1)
    prev_chunk = lambda i, k: jnp.where(i == 0, 0, (i - 1) * nk + k)
    if two_a:
        half = nk // 2
        a_specs = [pl.BlockSpec((tm, tk), lambda i, k: (tile(i), jnp.minimum(kch(i, k), half - 1))),
                   pl.BlockSpec((tm, tk), lambda i, k: (tile(i), jnp.maximum(kch(i, k) - half, 0)))]
        a_args = [a, a2]
    else:
        a_specs = [pl.BlockSpec((tm, tk), lambda i, k: (tile(i), kch(i, k)))]
        a_args = [a]
    mod_spec = lambda layer: pl.BlockSpec(
        (None, None, N_MOD, d), lambda i, k: (layer, group_of(jnp.maximum(i - 1, 0), tm), 0, 0))
    ln_spec = pl.BlockSpec((None, 3, d), lambda i, k: (li, 0, 0))
    x_spec = pl.BlockSpec((rc, d), lambda i, k: (jnp.maximum(i - 1, 0) * nk + k, 0))
    out_spec = pl.BlockSpec((rc, d), lambda i, k: (prev_chunk(i, k), 0))
    in_specs = a_specs + [_stacked((tk, d), wlead, lambda i, k: (kch(i, k), 0)), x_spec, mod_spec(li)]
    args = a_args + [w, x, mods]
    if next_row is not None:
        in_specs.append(mod_spec(lin))
        args.append(mods)
        out_specs = [out_spec, out_spec]
        out_shape = [jax.ShapeDtypeStruct((rows, d), F32), jax.ShapeDtypeStruct((rows, d), BF16)]
    else:
        out_specs = [out_spec]
        out_shape = [jax.ShapeDtypeStruct((rows, d), F32)]
    in_specs += [ln_spec, ln_spec]
    args += [post_g, post_b]
    kern = functools.partial(_mmres_kernel, nt=nt, nk=nk, gate_row=gate_row, gate_mul=gate_mul, alpha=alpha,
                             ln_row=ln_row, next_row=next_row, cn=min(1024, d), two_a=two_a)
    outs = pl.pallas_call(
        kern,
        grid=(nt + 1, nk),
        in_specs=in_specs,
        out_specs=out_specs,
        out_shape=out_shape,
        scratch_shapes=[pltpu.VMEM((tm, d), F32), pltpu.VMEM((tm, d), F32)],
        compiler_params=_cparams(("arbitrary", "arbitrary")),
        name="mm_res",
    )(*args)
    return (outs[0], outs[1]) if next_row is not None else (outs[0], None)


def _qkprep_kernel(p_ref, g_ref, cos_ref, sin_ref, o_ref):
    cos = cos_ref[...]
    sin = sin_ref[...]
    lane = lax.broadcasted_iota(jnp.int32, cos.shape, 1)
    lower = (lane & (HEAD_DIM // 4)) == 0
    for hd in range(p_ref.shape[1] // HEAD_DIM):
        sl = slice(hd * HEAD_DIM, (hd + 1) * HEAD_DIM)
        v = p_ref[:, sl].astype(F32)
        y = v * lax.rsqrt(jnp.mean(v * v, axis=-1, keepdims=True) + EPS) * g_ref[:, sl]
        partner = jnp.where(lower, pltpu.roll(y, HEAD_DIM - HEAD_DIM // 4, 1), pltpu.roll(y, HEAD_DIM // 4, 1))
        o_ref[:, sl] = (y * cos + partner * sin).astype(BF16)


def _qk_prep(p, gains, li, cos, sin, tm):
    t = p.shape[0]
    nqk = gains.shape[-1]
    heads = nqk // HEAD_DIM
    tn = HEAD_DIM * max(n for n in (1, 2, 4) if heads % n == 0)
    return pl.pallas_call(
        _qkprep_kernel,
        grid=(t // tm, nqk // tn),
        in_specs=[pl.BlockSpec((tm, tn), lambda i, j: (i, j)),
                  pl.BlockSpec((None, 1, tn), lambda i, j: (li, 0, j)),
                  pl.BlockSpec((tm, HEAD_DIM), lambda i, j: (i, 0)),
                  pl.BlockSpec((tm, HEAD_DIM), lambda i, j: (i, 0))],
        out_specs=pl.BlockSpec((tm, tn), lambda i, j: (i, j)),
        out_shape=jax.ShapeDtypeStruct((t, nqk), BF16),
        compiler_params=_cparams(("arbitrary", "arbitrary")),
        name="qk_prep",
    )(p, gains, cos, sin)


_QK_T = (((1,), (1,)), ((), ()))


def _attn_ctx_kernel(q_ref, kc_ref, vc_ref, _, o_ref):
    for g in range(Q_PER_KV):
        sl = slice(g * HEAD_DIM, (g + 1) * HEAD_DIM)
        s = lax.dot_general(q_ref[:, sl], kc_ref[...], _QK_T, preferred_element_type=F32)
        p = jnp.exp2(s - jnp.max(s, axis=-1, keepdims=True))
        o = jnp.dot(p.astype(BF16), vc_ref[...], preferred_element_type=F32)
        o_ref[:, sl] = (o / jnp.sum(p, axis=-1, keepdims=True)).astype(BF16)


def _attn_lat_kernel(q_ref, kc_ref, vc_ref, kl_ref, vl_ref, o_ref, s_scr, p_scr, l_scr):
    lc = kc_ref.shape[0]

    def scores(g):
        q = q_ref[:, g * HEAD_DIM:(g + 1) * HEAD_DIM]
        s_scr[g % 2, :, 0:lc] = lax.dot_general(q, kc_ref[...], _QK_T, preferred_element_type=F32)
        s_scr[g % 2, :, lc:] = lax.dot_general(q, kl_ref[...], _QK_T, preferred_element_type=F32)

    def softmax(g):
        s = s_scr[g % 2]
        p = jnp.exp2(s - jnp.max(s, axis=-1, keepdims=True))
        l_scr[g] = jnp.sum(p, axis=-1, keepdims=True)
        p_scr[g % 2] = p.astype(BF16)

    def values(g):
        o = jnp.dot(p_scr[g % 2, :, 0:lc], vc_ref[...], preferred_element_type=F32)
        o = o + jnp.dot(p_scr[g % 2, :, lc:], vl_ref[...], preferred_element_type=F32)
        o_ref[:, g * HEAD_DIM:(g + 1) * HEAD_DIM] = (o / l_scr[g]).astype(BF16)

    scores(0)
    for g in range(Q_PER_KV):
        if g + 1 < Q_PER_KV:
            scores(g + 1)
        softmax(g)
        values(g)


def _attention(qk, p, *, b, s, l, n_kv, tq, out_rows=None, o_buf=None):
    kcol = n_kv * Q_PER_KV
    vcol = kcol + n_kv
    ctx0 = b * s // l
    qw = Q_PER_KV * HEAD_DIM
    latent = o_buf is None
    if latent:
        nq = s // tq
        grid = (b, n_kv, nq)
        in_specs = [pl.BlockSpec((tq, qw), lambda bi, h, qi: (bi * nq + qi, h)),
                    pl.BlockSpec((l, HEAD_DIM), lambda bi, h, qi: (ctx0 + bi, kcol + h)),
                    pl.BlockSpec((l, HEAD_DIM), lambda bi, h, qi: (ctx0 + bi, vcol + h)),
                    pl.BlockSpec((s, HEAD_DIM), lambda bi, h, qi: (bi, kcol + h)),
                    pl.BlockSpec((s, HEAD_DIM), lambda bi, h, qi: (bi, vcol + h))]
        args = (qk, qk, p, qk, p)
        out_specs = pl.BlockSpec((tq, qw), lambda bi, h, qi: (bi * nq + qi, h))
        aliases = {}
        sem = ("arbitrary", "arbitrary", "arbitrary")
        kern = _attn_lat_kernel
        scratch = [pltpu.VMEM((2, tq, l + s), F32), pltpu.VMEM((2, tq, l + s), BF16),
                   pltpu.VMEM((Q_PER_KV, tq, 1), F32)]
    else:
        out_rows = o_buf.shape[0]
        grid = (b, n_kv)
        in_specs = [pl.BlockSpec((l, qw), lambda bi, h: (ctx0 + bi, h)),
                    pl.BlockSpec((l, HEAD_DIM), lambda bi, h: (ctx0 + bi, kcol + h)),
                    pl.BlockSpec((l, HEAD_DIM), lambda bi, h: (ctx0 + bi, vcol + h)),
                    pl.BlockSpec(memory_space=pl.ANY)]
        args = (qk, qk, p, o_buf)
        out_specs = pl.BlockSpec((l, qw), lambda bi, h: (ctx0 + bi, h))
        aliases = {3: 0}
        sem = ("arbitrary", "arbitrary")
        kern = _attn_ctx_kernel
        scratch = []
    return pl.pallas_call(
        kern,
        grid=grid,
        in_specs=in_specs,
        out_specs=out_specs,
        out_shape=jax.ShapeDtypeStruct((out_rows, n_kv * qw), BF16),
        scratch_shapes=scratch,
        input_output_aliases=aliases,
        compiler_params=_cparams(sem),
        name="attn_lat" if latent else "attn_ctx",
    )(*args)


def _conv_kernel(sb_ref, sc_ref, sx_ref, a_ref, gt_ref,
                 scp_ref, sxp_ref, ap_ref, gtp_ref, scn_ref, sxn_ref, an_ref, gtn_ref,
                 sw_ref, cw_ref, cb_ref, cg_ref, cbeta_ref, o_ref, z_scr, u_scr, part_scr,
                 *, ts, lat_rows, s, l, rc):
    c = sb_ref.shape[1]
    i = pl.program_id(0)
    row0 = i * ts
    is_ctx = row0 >= lat_rows
    pos = jnp.where(is_ctx, lax.rem(jnp.maximum(row0 - lat_rows, 0), l), lax.rem(row0, s))
    seqlen = jnp.where(is_ctx, l, s)
    keep_prev = (pos > 0).astype(F32)
    keep_next = (pos + ts < seqlen).astype(F32)

    def glu(a, g):
        return a.astype(F32) * jax.nn.sigmoid(g.astype(F32))

    z_scr[0:HALO, :] = scp_ref[...].astype(F32) * sxp_ref[...].astype(F32) * keep_prev
    z_scr[HALO:HALO + ts, :] = sc_ref[...].astype(F32) * sx_ref[...].astype(F32)
    z_scr[HALO + ts:2 * HALO + ts, :] = scn_ref[...].astype(F32) * sxn_ref[...].astype(F32) * keep_next
    u_scr[0:HALO, :] = glu(ap_ref[...], gtp_ref[...]) * keep_prev
    u_scr[HALO:HALO + ts, :] = glu(a_ref[...], gt_ref[...])
    u_scr[HALO + ts:2 * HALO + ts, :] = glu(an_ref[...], gtn_ref[...]) * keep_next

    spad = SCONV_WIDTH // 2
    cpad = CCONV_WIDTH // 2
    cw_tiles = [jnp.broadcast_to(cw_ref[k:k + 1, :], (SUBLANES, c)) for k in range(CCONV_WIDTH)]

    def scale_rows(v, w_tile):
        n = v.shape[0]
        return (v.reshape(n // SUBLANES, SUBLANES, c) * w_tile[None]).reshape(n, c)

    for r0 in range(0, ts, rc):
        base = HALO + r0
        acc = sw_ref[0:1, :] * z_scr[base - spad:base - spad + rc, :]
        for k in range(1, SCONV_WIDTH):
            acc = acc + sw_ref[k:k + 1, :] * z_scr[base - spad + k:base - spad + k + rc, :]
        o_ref[r0:r0 + rc, 0:c] = (sb_ref[r0:r0 + rc, :].astype(F32) * acc).astype(BF16)

        acc = None
        for res in range(SUBLANES):
            taps = [k for k in range(CCONV_WIDTH) if (HALO - cpad + k) % SUBLANES == res]
            n = rc if res == 0 else rc + SUBLANES
            part = None
            for k in taps:
                start = r0 + (HALO - cpad + k) - res
                term = scale_rows(u_scr[start:start + n, :], cw_tiles[k])
                part = term if part is None else part + term
            if res:
                part_scr[res - 1] = part
                part = part_scr[res - 1, res:res + rc, :]
            acc = part if acc is None else acc + part
        y = _ln_rows(acc + cb_ref[...]) * cg_ref[...] + cbeta_ref[...]
        o_ref[r0:r0 + rc, c:2 * c] = _silu(y).astype(BF16)


def _convs(p, li, sconv_w, cconv_w, cconv_b, cln_g, cln_b, *, rows, b, s, l, ts):
    t = p.shape[0]
    c = sconv_w.shape[-1]
    depth = sconv_w.shape[0]
    hb = ts // HALO
    nhb = t // HALO
    rc = 32
    main = lambda col: pl.BlockSpec((ts, c), lambda i: (i, col))
    prev = lambda col: pl.BlockSpec((HALO, c), lambda i: (jnp.maximum(i * hb - 1, 0), col))
    nxt = lambda col: pl.BlockSpec((HALO, c), lambda i: (jnp.minimum((i + 1) * hb, nhb - 1), col))
    vec = lambda r: pl.BlockSpec((None, r, c), lambda i: (li, 0, 0))
    in_specs = ([main(3), main(4), main(5), main(6), main(7)]
                + [prev(4), prev(5), prev(6), prev(7)] + [nxt(4), nxt(5), nxt(6), nxt(7)]
                + [vec(SCONV_WIDTH), vec(CCONV_WIDTH), vec(1), vec(1), vec(1)])
    kern = functools.partial(_conv_kernel, ts=ts, lat_rows=b * s, s=s, l=l, rc=rc)
    return pl.pallas_call(
        kern,
        grid=(rows // ts,),
        in_specs=in_specs,
        out_specs=pl.BlockSpec((ts, 2 * c), lambda i: (i, 0)),
        out_shape=jax.ShapeDtypeStruct((rows, 2 * c), BF16),
        scratch_shapes=[pltpu.VMEM((ts + 2 * HALO, c), F32), pltpu.VMEM((ts + 2 * HALO, c), F32),
                        pltpu.VMEM((SUBLANES - 1, rc + SUBLANES, c), F32)],
        compiler_params=_cparams(("arbitrary",)),
        name="convs",
    )(*([p] * 13), sconv_w, cconv_w, cconv_b.reshape(depth, 1, c), cln_g.reshape(depth, 1, c),
      cln_b.reshape(depth, 1, c))


def _rope_tables(b, s, l):
    rows = s // GRID_W
    row = jnp.repeat(jnp.arange(rows, dtype=F32), GRID_W)
    col = jnp.tile(jnp.arange(GRID_W, dtype=F32), rows)
    axis_dim = HEAD_DIM // 2
    inv_freq = ROPE_THETA ** (-jnp.arange(0, axis_dim, 2, dtype=F32) / axis_dim)
    ar, ac = row[:, None] * inv_freq, col[:, None] * inv_freq
    cos = jnp.concatenate([jnp.cos(ar), jnp.cos(ar), jnp.cos(ac), jnp.cos(ac)], -1)
    sin = jnp.concatenate([-jnp.sin(ar), jnp.sin(ar), -jnp.sin(ac), jnp.sin(ac)], -1)
    cos = jnp.concatenate([jnp.tile(cos, (b, 1)), jnp.ones((b * l, HEAD_DIM), F32)], 0)
    sin = jnp.concatenate([jnp.tile(sin, (b, 1)), jnp.zeros((b * l, HEAD_DIM), F32)], 0)
    return cos, sin


def kernel(x, c, ctx, c_ctx, w_ada, b_ada, post_ln_g, post_ln_b, ffn_w_gu, ffn_w_down, w_in, w_out, q_norm_g,
           k_norm_g, sconv_w, cconv_w, cconv_b, cconv_ln_g, cconv_ln_b):
    b, s, d = x.shape
    l = ctx.shape[1]
    depth = w_ada.shape[0]
    n_q = d // (2 * HEAD_DIM)
    n_kv = n_q // Q_PER_KV
    alpha = (2.0 * depth) ** 0.25
    q_scale = HEAD_DIM ** -0.5 * math.log2(math.e)
    lat = b * s
    t = lat + b * l
    tm = min(1024, s, b * l)
    ts = min(256, l)
    tmr = tm
    tk = 512
    assert s % tm == 0 and (b * l) % tm == 0 and s % ts == 0 and l % ts == 0 and s % l == 0
    assert ts % HALO == 0 and HALO > CCONV_WIDTH // 2

    def group_of(i, tile):
        return jnp.minimum(i * tile // s, b)

    xs = jnp.concatenate([x.reshape(lat, d), ctx.reshape(b * l, d)], 0)
    cvec = jnp.zeros((8, d), F32).at[:b].set(c).at[b].set(c_ctx)
    mods = _adaln(cvec, w_ada, b_ada).reshape(depth, 8, N_MOD, d)
    cos, sin = _rope_tables(b, s, l)
    gains = jnp.concatenate([jnp.tile(q_norm_g * q_scale, (1, n_q)), jnp.tile(k_norm_g, (1, n_kv))], 1)[:, None, :]
    w_gu16, w_dn16 = ffn_w_gu.astype(BF16), ffn_w_down.astype(BF16)
    w_in16, w_out16 = w_in.astype(BF16), w_out.astype(BF16)
    res = functools.partial(_mm_res, mods=mods, post_g=post_ln_g, post_b=post_ln_b, group_of=group_of, alpha=alpha,
                            tm=tmr, tk=tk)

    h = _lnmod(xs, mods, 0, group_of, 0, min(512, tm))
    for li in range(depth):
        last = li == depth - 1
        rows = lat if last else t

        a = _mm_glu(h, w_gu16, (li, 0), t, tm, 512)
        xs, h = res(a, w_dn16, (li, 0), xs, li=li, lin=li, rows=t, gate_row=2, gate_mul=0.5, ln_row=0, next_row=3)

        p = _mm_plain(h, w_in16, (li,), tm, 512)
        qk = _qk_prep(p, gains, li, cos, sin, tm)
        y_attn = _attention(qk, p, b=b, s=s, l=l, n_kv=n_kv, tq=min(512, s), out_rows=rows)
        if not last:
            y_attn = _attention(qk, p, b=b, s=s, l=l, n_kv=n_kv, tq=l, o_buf=y_attn)
        y_conv = _convs(p, li, sconv_w, cconv_w, cconv_b, cconv_ln_g, cconv_ln_b, rows=rows, b=b, s=s, l=l, ts=ts)
        xs, h = res(y_attn, w_out16, (li,), xs, li=li, lin=li, rows=rows, gate_row=5, gate_mul=1.0, ln_row=1,
                    next_row=6, a2=y_conv)

        a = _mm_glu(h, w_gu16, (li, 1), rows, tm, 512)
        xs, h = res(a, w_dn16, (li, 1), xs, li=li, lin=li + 1, rows=rows, gate_row=8, gate_mul=0.5, ln_row=2,
                    next_row=None if last else 0)
    return xs.reshape(b, s, d)
```

```python
import functools
import math

import jax
import jax.numpy as jnp
from jax import lax
from jax.experimental import pallas as pl
from jax.experimental.pallas import tpu as pltpu

F32 = jnp.float32
BF16 = jnp.bfloat16

HEAD_DIM = 128
Q_PER_KV = 4
GRID_W = 64
ROPE_THETA = 10000.0
SCONV_WIDTH = 3
CCONV_WIDTH = 31
N_MOD = 9
EPS = 1e-6
SUBLANES = 8
EPI_ROWS = 8
HALO = 16
V7X_VMEM_LIMIT = 60000 * 1024


def _cparams(sem):
    return pltpu.CompilerParams(dimension_semantics=sem, vmem_limit_bytes=V7X_VMEM_LIMIT)


def _silu(v):
    return v * jax.nn.sigmoid(v)


def _ln_rows(z):
    mu = jnp.mean(z, axis=-1, keepdims=True)
    zc = z - mu
    var = jnp.mean(zc * zc, axis=-1, keepdims=True)
    return zc * lax.rsqrt(var + EPS)


def _stacked(block, lead, index):
    return pl.BlockSpec((None,) * len(lead) + block, lambda *g: lead + index(*g))


def _adaln_kernel(c_ref, w_ref, b_ref, o_ref):
    sc = _silu(c_ref[...]).astype(BF16)
    o_ref[...] = jnp.dot(sc, w_ref[...].astype(BF16), preferred_element_type=F32) + b_ref[...]


def _adaln(cvec, w_ada, b_ada, tn=512):
    depth, d, n = w_ada.shape
    return pl.pallas_call(
        _adaln_kernel,
        grid=(depth, n // tn),
        in_specs=[pl.BlockSpec((8, d), lambda l, j: (0, 0)),
                  pl.BlockSpec((None, d, tn), lambda l, j: (l, 0, j)),
                  pl.BlockSpec((None, 1, tn), lambda l, j: (l, 0, j))],
        out_specs=pl.BlockSpec((None, 8, tn), lambda l, j: (l, 0, j)),
        out_shape=jax.ShapeDtypeStruct((depth, 8, n), F32),
        compiler_params=_cparams(("arbitrary", "arbitrary")),
        name="adaln",
    )(cvec, w_ada, b_ada.reshape(depth, 1, n))


def _lnmod_kernel(x_ref, mod_ref, h_ref, *, shift_row, rows):
    shift = mod_ref[shift_row:shift_row + 1, :]
    scale1 = 1.0 + mod_ref[shift_row + 1:shift_row + 2, :]

    def body(r, carry):
        sl = pl.ds(pl.multiple_of(r * rows, rows), rows)
        h_ref[sl, :] = (_ln_rows(x_ref[sl, :]) * scale1 + shift).astype(BF16)
        return carry

    lax.fori_loop(0, x_ref.shape[0] // rows, body, 0)


def _lnmod(x, mods, li, group_of, shift_row, tm):
    t, d = x.shape
    return pl.pallas_call(
        functools.partial(_lnmod_kernel, shift_row=shift_row, rows=64),
        grid=(t // tm,),
        in_specs=[pl.BlockSpec((tm, d), lambda i: (i, 0)),
                  pl.BlockSpec((None, None, N_MOD, d), lambda i: (li, group_of(i, tm), 0, 0))],
        out_specs=pl.BlockSpec((tm, d), lambda i: (i, 0)),
        out_shape=jax.ShapeDtypeStruct((t, d), BF16),
        compiler_params=_cparams(("arbitrary",)),
        name="lnmod",
    )(x, mods)


def _glu_kernel(h_ref, wg_ref, wu_ref, o_ref):
    h = h_ref[...]
    g = jnp.dot(h, wg_ref[...], preferred_element_type=F32)
    u = jnp.dot(h, wu_ref[...], preferred_element_type=F32)
    o_ref[...] = (_silu(g) * u).astype(BF16)


def _mm_glu(h, w_gu, lead, rows, tm, tn):
    d = h.shape[1]
    f = w_gu.shape[-1] // 2
    nj = f // tn
    return pl.pallas_call(
        _glu_kernel,
        grid=(rows // tm, nj),
        in_specs=[pl.BlockSpec((tm, d), lambda i, j: (i, 0)),
                  _stacked((d, tn), lead, lambda i, j: (0, j)),
                  _stacked((d, tn), lead, lambda i, j: (0, j + nj))],
        out_specs=pl.BlockSpec((tm, tn), lambda i, j: (i, j)),
        out_shape=jax.ShapeDtypeStruct((rows, f), BF16),
        compiler_params=_cparams(("arbitrary", "arbitrary")),
        name="mm_glu",
    )(h, w_gu, w_gu)


def _plain_kernel(h_ref, w_ref, o_ref):
    o_ref[...] = jnp.dot(h_ref[...], w_ref[...], preferred_element_type=F32).astype(BF16)


def _mm_plain(h, w, lead, tm, tn):
    t, d = h.shape
    n = w.shape[-1]
    return pl.pallas_call(
        _plain_kernel,
        grid=(t // tm, n // tn),
        in_specs=[pl.BlockSpec((tm, d), lambda i, j: (i, 0)),
                  _stacked((d, tn), lead, lambda i, j: (0, j))],
        out_specs=pl.BlockSpec((tm, tn), lambda i, j: (i, j)),
        out_shape=jax.ShapeDtypeStruct((t, n), BF16),
        compiler_params=_cparams(("arbitrary", "arbitrary")),
        name="mm_in",
    )(h, w)


def _mmres_kernel(*refs, nt, nk, gate_row, gate_mul, alpha, ln_row, next_row, cn, two_a):
    refs = list(refs)
    a_ref = refs.pop(0)
    a2_ref = refs.pop(0) if two_a else None
    w_ref, x_ref, mod_ref = refs[:3]
    refs = refs[3:]
    modn_ref = refs.pop(0) if next_row is not None else None
    lng_ref, lnb_ref, xo_ref = refs[:3]
    refs = refs[3:]
    ho_ref = refs.pop(0) if next_row is not None else None
    acc_even, acc_odd = refs
    i = pl.program_id(0)
    k = pl.program_id(1)
    rc, n = x_ref.shape

    @pl.when((i == 0) & (k == 0))
    def _():
        acc_even[...] = jnp.zeros_like(acc_even)
        acc_odd[...] = jnp.zeros_like(acc_odd)

    def matmul(acc):
        a = jnp.where(k < nk // 2, a_ref[...], a2_ref[...]) if two_a else a_ref[...]
        for c in range(n // cn):
            sl = slice(c * cn, (c + 1) * cn)
            acc[:, sl] += jnp.dot(a, w_ref[:, sl], preferred_element_type=F32)

    def finish_rows(acc):
        gate = mod_ref[gate_row:gate_row + 1, :] * gate_mul
        g = lng_ref[ln_row:ln_row + 1, :]
        b = lnb_ref[ln_row:ln_row + 1, :]
        if next_row is not None:
            shift = modn_ref[next_row:next_row + 1, :]
            scale1 = 1.0 + modn_ref[next_row + 1:next_row + 2, :]
        for sub in range(rc // EPI_ROWS):
            blk = slice(sub * EPI_ROWS, (sub + 1) * EPI_ROWS)
            rs = pl.ds(pl.multiple_of(k * rc + sub * EPI_ROWS, EPI_ROWS), EPI_ROWS)
            y = _ln_rows(alpha * x_ref[blk, :] + gate * acc[rs, :]) * g + b
            acc[rs, :] = jnp.zeros((EPI_ROWS, n), F32)
            xo_ref[blk, :] = y
            if next_row is not None:
                ho_ref[blk, :] = (_ln_rows(y) * scale1 + shift).astype(BF16)

    for parity, (cur, prev) in enumerate(((acc_even, acc_odd), (acc_odd, acc_even))):
        @pl.when((i < nt) & (lax.rem(i, 2) == parity))
        def _(cur=cur, prev=prev):
            matmul(cur)
            finish_rows(prev)

    @pl.when(i == nt)
    def _():
        finish_rows(acc_even if (nt - 1) % 2 == 0 else acc_odd)


def _mm_res(a, w, wlead, x, mods, li, lin, post_g, post_b, group_of, *, rows, gate_row, gate_mul, alpha, ln_row,
            next_row, tm, tk, a2=None):
    d = x.shape[1]
    kdim = w.shape[-2]
    nk = kdim // tk
    nt = rows // tm
    rc = tm // nk
    assert tm % nk == 0 and rc % EPI_ROWS == 0
    two_a = a2 is not None
    tile = lambda i: jnp.minimum(i, nt - 1)
    kch = lambda i, k: jnp.where(i < nt, k, nk - 1)
    prev_chunk = lambda i, k: jnp.where(i == 0, 0, (i - 1) * nk + k)
    if two_a:
        half = nk // 2
        a_specs = [pl.BlockSpec((tm, tk), lambda i, k: (tile(i), jnp.minimum(kch(i, k), half - 1))),
                   pl.BlockSpec((tm, tk), lambda i, k: (tile(i), jnp.maximum(kch(i, k) - half, 0)))]
        a_args = [a, a2]
    else:
        a_specs = [pl.BlockSpec((tm, tk), lambda i, k: (tile(i), kch(i, k)))]
        a_args = [a]
    mod_spec = lambda layer: pl.BlockSpec(
        (None, None, N_MOD, d), lambda i, k: (layer, group_of(jnp.maximum(i - 1, 0), tm), 0, 0))
    ln_spec = pl.BlockSpec((None, 3, d), lambda i, k: (li, 0, 0))
    x_spec = pl.BlockSpec((rc, d), lambda i, k: (jnp.maximum(i - 1, 0) * nk + k, 0))
    out_spec = pl.BlockSpec((rc, d), lambda i, k: (prev_chunk(i, k), 0))
    in_specs = a_specs + [_stacked((tk, d), wlead, lambda i, k: (kch(i, k), 0)), x_spec, mod_spec(li)]
    args = a_args + [w, x, mods]
    if next_row is not None:
        in_specs.append(mod_spec(lin))
        args.append(mods)
        out_specs = [out_spec, out_spec]
        out_shape = [jax.ShapeDtypeStruct((rows, d), F32), jax.ShapeDtypeStruct((rows, d), BF16)]
    else:
        out_specs = [out_spec]
        out_shape = [jax.ShapeDtypeStruct((rows, d), F32)]
    in_specs += [ln_spec, ln_spec]
    args += [post_g, post_b]
    kern = functools.partial(_mmres_kernel, nt=nt, nk=nk, gate_row=gate_row, gate_mul=gate_mul, alpha=alpha,
                             ln_row=ln_row, next_row=next_row, cn=min(1024, d), two_a=two_a)
    outs = pl.pallas_call(
        kern,
        grid=(nt + 1, nk),
        in_specs=in_specs,
        out_specs=out_specs,
        out_shape=out_shape,
        scratch_shapes=[pltpu.VMEM((tm, d), F32), pltpu.VMEM((tm, d), F32)],
        compiler_params=_cparams(("arbitrary", "arbitrary")),
        name="mm_res",
    )(*args)
    return (outs[0], outs[1]) if next_row is not None else (outs[0], None)


def _qkprep_kernel(p_ref, g_ref, cos_ref, sin_ref, o_ref):
    cos = cos_ref[...]
    sin = sin_ref[...]
    lane = lax.broadcasted_iota(jnp.int32, cos.shape, 1)
    lower = (lane & (HEAD_DIM // 4)) == 0
    for hd in range(p_ref.shape[1] // HEAD_DIM):
        sl = slice(hd * HEAD_DIM, (hd + 1) * HEAD_DIM)
        v = p_ref[:, sl].astype(F32)
        y = v * lax.rsqrt(jnp.mean(v * v, axis=-1, keepdims=True) + EPS) * g_ref[:, sl]
        partner = jnp.where(lower, pltpu.roll(y, HEAD_DIM - HEAD_DIM // 4, 1), pltpu.roll(y, HEAD_DIM // 4, 1))
        o_ref[:, sl] = (y * cos + partner * sin).astype(BF16)


def _qk_prep(p, gains, li, cos, sin, tm):
    t = p.shape[0]
    nqk = gains.shape[-1]
    heads = nqk // HEAD_DIM
    tn = HEAD_DIM * max(n for n in (1, 2, 4) if heads % n == 0)
    return pl.pallas_call(
        _qkprep_kernel,
        grid=(t // tm, nqk // tn),
        in_specs=[pl.BlockSpec((tm, tn), lambda i, j: (i, j)),
                  pl.BlockSpec((None, 1, tn), lambda i, j: (li, 0, j)),
                  pl.BlockSpec((tm, HEAD_DIM), lambda i, j: (i, 0)),
                  pl.BlockSpec((tm, HEAD_DIM), lambda i, j: (i, 0))],
        out_specs=pl.BlockSpec((tm, tn), lambda i, j: (i, j)),
        out_shape=jax.ShapeDtypeStruct((t, nqk), BF16),
        compiler_params=_cparams(("arbitrary", "arbitrary")),
        name="qk_prep",
    )(p, gains, cos, sin)


_QK_T = (((1,), (1,)), ((), ()))


def _attn_ctx_kernel(q_ref, kc_ref, vc_ref, _, o_ref):
    for g in range(Q_PER_KV):
        sl = slice(g * HEAD_DIM, (g + 1) * HEAD_DIM)
        s = lax.dot_general(q_ref[:, sl], kc_ref[...], _QK_T, preferred_element_type=F32)
        p = jnp.exp2(s - jnp.max(s, axis=-1, keepdims=True))
        o = jnp.dot(p.astype(BF16), vc_ref[...], preferred_element_type=F32)
        o_ref[:, sl] = (o / jnp.sum(p, axis=-1, keepdims=True)).astype(BF16)


def _attn_lat_kernel(q_ref, kc_ref, vc_ref, kl_ref, vl_ref, o_ref, s_scr, p_scr, k_all, v_all):
    lc = kc_ref.shape[0]

    @pl.when(pl.program_id(2) == 0)
    def _():
        k_all[0:lc, :] = kc_ref[...]
        k_all[lc:, :] = kl_ref[...]
        v_all[0:lc, 0:HEAD_DIM] = vc_ref[...]
        v_all[lc:, 0:HEAD_DIM] = vl_ref[...]
        lane = lax.broadcasted_iota(jnp.int32, (v_all.shape[0], HEAD_DIM), 1)
        v_all[:, HEAD_DIM:] = jnp.where(lane == 0, 1.0, 0.0).astype(BF16)

    def scores(g):
        q = q_ref[:, g * HEAD_DIM:(g + 1) * HEAD_DIM]
        s_scr[g % 2] = lax.dot_general(q, k_all[...], _QK_T, preferred_element_type=F32)

    def softmax(g):
        s = s_scr[g % 2]
        p_scr[g % 2] = jnp.exp2(s - jnp.max(s, axis=-1, keepdims=True)).astype(BF16)

    def values(g):
        o = jnp.dot(p_scr[g % 2], v_all[...], preferred_element_type=F32)
        o_ref[:, g * HEAD_DIM:(g + 1) * HEAD_DIM] = (o[:, 0:HEAD_DIM] / o[:, HEAD_DIM:HEAD_DIM + 1]).astype(BF16)

    scores(0)
    for g in range(Q_PER_KV):
        if g + 1 < Q_PER_KV:
            scores(g + 1)
        softmax(g)
        values(g)


def _attention(qk, p, *, b, s, l, n_kv, tq, out_rows=None, o_buf=None):
    kcol = n_kv * Q_PER_KV
    vcol = kcol + n_kv
    ctx0 = b * s // l
    qw = Q_PER_KV * HEAD_DIM
    latent = o_buf is None
    if latent:
        nq = s // tq
        grid = (b, n_kv, nq)
        in_specs = [pl.BlockSpec((tq, qw), lambda bi, h, qi: (bi * nq + qi, h)),
                    pl.BlockSpec((l, HEAD_DIM), lambda bi, h, qi: (ctx0 + bi, kcol + h)),
                    pl.BlockSpec((l, HEAD_DIM), lambda bi, h, qi: (ctx0 + bi, vcol + h)),
                    pl.BlockSpec((s, HEAD_DIM), lambda bi, h, qi: (bi, kcol + h)),
                    pl.BlockSpec((s, HEAD_DIM), lambda bi, h, qi: (bi, vcol + h))]
        args = (qk, qk, p, qk, p)
        out_specs = pl.BlockSpec((tq, qw), lambda bi, h, qi: (bi * nq + qi, h))
        aliases = {}
        sem = ("arbitrary", "arbitrary", "arbitrary")
        kern = _attn_lat_kernel
        scratch = [pltpu.VMEM((2, tq, l + s), F32), pltpu.VMEM((2, tq, l + s), BF16),
                   pltpu.VMEM((l + s, HEAD_DIM), BF16), pltpu.VMEM((l + s, 2 * HEAD_DIM), BF16)]
    else:
        out_rows = o_buf.shape[0]
        grid = (b, n_kv)
        in_specs = [pl.BlockSpec((l, qw), lambda bi, h: (ctx0 + bi, h)),
                    pl.BlockSpec((l, HEAD_DIM), lambda bi, h: (ctx0 + bi, kcol + h)),
                    pl.BlockSpec((l, HEAD_DIM), lambda bi, h: (ctx0 + bi, vcol + h)),
                    pl.BlockSpec(memory_space=pl.ANY)]
        args = (qk, qk, p, o_buf)
        out_specs = pl.BlockSpec((l, qw), lambda bi, h: (ctx0 + bi, h))
        aliases = {3: 0}
        sem = ("arbitrary", "arbitrary")
        kern = _attn_ctx_kernel
        scratch = []
    return pl.pallas_call(
        kern,
        grid=grid,
        in_specs=in_specs,
        out_specs=out_specs,
        out_shape=jax.ShapeDtypeStruct((out_rows, n_kv * qw), BF16),
        scratch_shapes=scratch,
        input_output_aliases=aliases,
        compiler_params=_cparams(sem),
        name="attn_lat" if latent else "attn_ctx",
    )(*args)


def _conv_kernel(sb_ref, sc_ref, sx_ref, a_ref, gt_ref,
                 scp_ref, sxp_ref, ap_ref, gtp_ref, scn_ref, sxn_ref, an_ref, gtn_ref,
                 sw_ref, cw_ref, cb_ref, cg_ref, cbeta_ref, o_ref, z_scr, u_scr, part_scr,
                 *, ts, lat_rows, s, l, rc):
    c = sb_ref.shape[1]
    i = pl.program_id(0)
    row0 = i * ts
    is_ctx = row0 >= lat_rows
    pos = jnp.where(is_ctx, lax.rem(jnp.maximum(row0 - lat_rows, 0), l), lax.rem(row0, s))
    seqlen = jnp.where(is_ctx, l, s)
    keep_prev = (pos > 0).astype(F32)
    keep_next = (pos + ts < seqlen).astype(F32)

    def glu(a, g):
        return a.astype(F32) * jax.nn.sigmoid(g.astype(F32))

    z_scr[0:HALO, :] = scp_ref[...].astype(F32) * sxp_ref[...].astype(F32) * keep_prev
    z_scr[HALO:HALO + ts, :] = sc_ref[...].astype(F32) * sx_ref[...].astype(F32)
    z_scr[HALO + ts:2 * HALO + ts, :] = scn_ref[...].astype(F32) * sxn_ref[...].astype(F32) * keep_next
    u_scr[0:HALO, :] = glu(ap_ref[...], gtp_ref[...]) * keep_prev
    u_scr[HALO:HALO + ts, :] = glu(a_ref[...], gt_ref[...])
    u_scr[HALO + ts:2 * HALO + ts, :] = glu(an_ref[...], gtn_ref[...]) * keep_next

    spad = SCONV_WIDTH // 2
    cpad = CCONV_WIDTH // 2
    cw_tiles = [jnp.broadcast_to(cw_ref[k:k + 1, :], (SUBLANES, c)) for k in range(CCONV_WIDTH)]

    def scale_rows(v, w_tile):
        n = v.shape[0]
        return (v.reshape(n // SUBLANES, SUBLANES, c) * w_tile[None]).reshape(n, c)

    for r0 in range(0, ts, rc):
        base = HALO + r0
        acc = sw_ref[0:1, :] * z_scr[base - spad:base - spad + rc, :]
        for k in range(1, SCONV_WIDTH):
            acc = acc + sw_ref[k:k + 1, :] * z_scr[base - spad + k:base - spad + k + rc, :]
        o_ref[r0:r0 + rc, 0:c] = (sb_ref[r0:r0 + rc, :].astype(F32) * acc).astype(BF16)

        acc = None
        for res in range(SUBLANES):
            taps = [k for k in range(CCONV_WIDTH) if (HALO - cpad + k) % SUBLANES == res]
            n = rc if res == 0 else rc + SUBLANES
            part = None
            for k in taps:
                start = r0 + (HALO - cpad + k) - res
                term = scale_rows(u_scr[start:start + n, :], cw_tiles[k])
                part = term if part is None else part + term
            if res:
                part_scr[res - 1] = part
                part = part_scr[res - 1, res:res + rc, :]
            acc = part if acc is None else acc + part
        y = _ln_rows(acc + cb_ref[...]) * cg_ref[...] + cbeta_ref[...]
        o_ref[r0:r0 + rc, c:2 * c] = _silu(y).astype(BF16)


def _convs(p, li, sconv_w, cconv_w, cconv_b, cln_g, cln_b, *, rows, b, s, l, ts):
    t = p.shape[0]
    c = sconv_w.shape[-1]
    depth = sconv_w.shape[0]
    hb = ts // HALO
    nhb = t // HALO
    rc = 32
    main = lambda col: pl.BlockSpec((ts, c), lambda i: (i, col))
    prev = lambda col: pl.BlockSpec((HALO, c), lambda i: (jnp.maximum(i * hb - 1, 0), col))
    nxt = lambda col: pl.BlockSpec((HALO, c), lambda i: (jnp.minimum((i + 1) * hb, nhb - 1), col))
    vec = lambda r: pl.BlockSpec((None, r, c), lambda i: (li, 0, 0))
    in_specs = ([main(3), main(4), main(5), main(6), main(7)]
                + [prev(4), prev(5), prev(6), prev(7)] + [nxt(4), nxt(5), nxt(6), nxt(7)]
                + [vec(SCONV_WIDTH), vec(CCONV_WIDTH), vec(1), vec(1), vec(1)])
    kern = functools.partial(_conv_kernel, ts=ts, lat_rows=b * s, s=s, l=l, rc=rc)
    return pl.pallas_call(
        kern,
        grid=(rows // ts,),
        in_specs=in_specs,
        out_specs=pl.BlockSpec((ts, 2 * c), lambda i: (i, 0)),
        out_shape=jax.ShapeDtypeStruct((rows, 2 * c), BF16),
        scratch_shapes=[pltpu.VMEM((ts + 2 * HALO, c), F32), pltpu.VMEM((ts + 2 * HALO, c), F32),
                        pltpu.VMEM((SUBLANES - 1, rc + SUBLANES, c), F32)],
        compiler_params=_cparams(("arbitrary",)),
        name="convs",
    )(*([p] * 13), sconv_w, cconv_w, cconv_b.reshape(depth, 1, c), cln_g.reshape(depth, 1, c),
      cln_b.reshape(depth, 1, c))


def _rope_tables(b, s, l):
    rows = s // GRID_W
    row = jnp.repeat(jnp.arange(rows, dtype=F32), GRID_W)
    col = jnp.tile(jnp.arange(GRID_W, dtype=F32), rows)
    axis_dim = HEAD_DIM // 2
    inv_freq = ROPE_THETA ** (-jnp.arange(0, axis_dim, 2, dtype=F32) / axis_dim)
    ar, ac = row[:, None] * inv_freq, col[:, None] * inv_freq
    cos = jnp.concatenate([jnp.cos(ar), jnp.cos(ar), jnp.cos(ac), jnp.cos(ac)], -1)
    sin = jnp.concatenate([-jnp.sin(ar), jnp.sin(ar), -jnp.sin(ac), jnp.sin(ac)], -1)
    cos = jnp.concatenate([jnp.tile(cos, (b, 1)), jnp.ones((b * l, HEAD_DIM), F32)], 0)
    sin = jnp.concatenate([jnp.tile(sin, (b, 1)), jnp.zeros((b * l, HEAD_DIM), F32)], 0)
    return cos, sin


def kernel(x, c, ctx, c_ctx, w_ada, b_ada, post_ln_g, post_ln_b, ffn_w_gu, ffn_w_down, w_in, w_out, q_norm_g,
           k_norm_g, sconv_w, cconv_w, cconv_b, cconv_ln_g, cconv_ln_b):
    b, s, d = x.shape
    l = ctx.shape[1]
    depth = w_ada.shape[0]
    n_q = d // (2 * HEAD_DIM)
    n_kv = n_q // Q_PER_KV
    alpha = (2.0 * depth) ** 0.25
    q_scale = HEAD_DIM ** -0.5 * math.log2(math.e)
    lat = b * s
    t = lat + b * l
    tm = min(1024, s, b * l)
    ts = min(256, l)
    tmr = tm
    tk = 512
    assert s % tm == 0 and (b * l) % tm == 0 and s % ts == 0 and l % ts == 0 and s % l == 0
    assert ts % HALO == 0 and HALO > CCONV_WIDTH // 2

    def group_of(i, tile):
        return jnp.minimum(i * tile // s, b)

    xs = jnp.concatenate([x.reshape(lat, d), ctx.reshape(b * l, d)], 0)
    cvec = jnp.zeros((8, d), F32).at[:b].set(c).at[b].set(c_ctx)
    mods = _adaln(cvec, w_ada, b_ada).reshape(depth, 8, N_MOD, d)
    cos, sin = _rope_tables(b, s, l)
    gains = jnp.concatenate([jnp.tile(q_norm_g * q_scale, (1, n_q)), jnp.tile(k_norm_g, (1, n_kv))], 1)[:, None, :]
    w_gu16, w_dn16 = ffn_w_gu.astype(BF16), ffn_w_down.astype(BF16)
    w_in16, w_out16 = w_in.astype(BF16), w_out.astype(BF16)
    res = functools.partial(_mm_res, mods=mods, post_g=post_ln_g, post_b=post_ln_b, group_of=group_of, alpha=alpha,
                            tm=tmr, tk=tk)

    h = _lnmod(xs, mods, 0, group_of, 0, min(512, tm))
    for li in range(depth):
        last = li == depth - 1
        rows = lat if last else t

        a = _mm_glu(h, w_gu16, (li, 0), t, tm, 512)
        xs, h = res(a, w_dn16, (li, 0), xs, li=li, lin=li, rows=t, gate_row=2, gate_mul=0.5, ln_row=0, next_row=3)

        p = _mm_plain(h, w_in16, (li,), tm, 512)
        qk = _qk_prep(p, gains, li, cos, sin, tm)
        y_attn = _attention(qk, p, b=b, s=s, l=l, n_kv=n_kv, tq=min(512, s), out_rows=rows)
        if not last:
            y_attn = _attention(qk, p, b=b, s=s, l=l, n_kv=n_kv, tq=l, o_buf=y_attn)
        y_conv = _convs(p, li, sconv_w, cconv_w, cconv_b, cconv_ln_g, cconv_ln_b, rows=rows, b=b, s=s, l=l, ts=ts)
        xs, h = res(y_attn, w_out16, (li,), xs, li=li, lin=li, rows=rows, gate_row=5, gate_mul=1.0, ln_row=1,
                    next_row=6, a2=y_conv)

        a = _mm_glu(h, w_gu16, (li, 1), rows, tm, 512)
        xs, h = res(a, w_dn16, (li, 1), xs, li=li, lin=li + 1, rows=rows, gate_row=8, gate_mul=0.5, ln_row=2,
                    next_row=None if last else 0)
    return xs.reshape(b, s, d)
```
